```python
import math
import jax
import jax.numpy as jnp
from jax import lax
import numpy as np

D_MODEL = 2048
BATCH = 1
SEQ = 16384
DEPTH = 1
DEC_BATCH = 16
DEC_SEQ = 2048
PAST_LEN = 128

MIX_WIDTH = D_MODEL
MLSTM_WIDTH = MIX_WIDTH // 2
N_MLSTM_HEADS = 4
MLSTM_HEAD_DIM = MLSTM_WIDTH // N_MLSTM_HEADS
FOURIER_WIDTH = MIX_WIDTH - MLSTM_WIDTH
N_FOURIER_GROUPS = 4
FOURIER_GROUP_DIM = FOURIER_WIDTH // N_FOURIER_GROUPS
N_GATES = 4 * N_MLSTM_HEADS
PROJ_WIDTH = 4 * MLSTM_WIDTH + N_GATES + FOURIER_WIDTH
CONV_WIDTH = 5
CHUNK = 128
N_EXPERTS = 32
TOP_K = 4
D_FF = D_MODEL
SWIGLU_LIMIT = 7.0
SWIGLU_ALPHA = 1.702
MOE_BLOCK = 256
EPS = 1e-6

kernel_name = 'hymba_mlstm_fnet_moe_adaln_encoder'


def rmsnorm(x, g):
    xf = x.astype(jnp.float32)
    y = xf * lax.rsqrt(jnp.mean(xf * xf, axis=-1, keepdims=True) + EPS)
    return (y * g.astype(jnp.float32)).astype(x.dtype)


def centred_conv(x, w, b):
    pad = CONV_WIDTH // 2
    s = x.shape[1]
    xp = jnp.pad(x, ((0, 0), (pad, pad), (0, 0)))
    out = xp[:, 0:s] * w[0]
    for j in range(1, CONV_WIDTH):
        out = out + xp[:, j:j + s] * w[j]
    return out + b


def _mlstm_chunk_step(carry, xs):
    c_state, n_state, m_state = carry
    q, k, v, li, lf = xs
    length = q.shape[2]
    mask = jnp.tril(jnp.ones((length, length), dtype=bool))
    b = jnp.cumsum(lf, axis=-1)
    g = b[..., -1]
    a = b + m_state[..., None]
    d_mat = jnp.where(mask, b[..., :, None] - b[..., None, :] + li[..., None, :], -jnp.inf)
    m_t = jnp.maximum(a, jnp.max(d_mat, axis=-1))
    w_mat = jnp.exp(d_mat - m_t[..., None]) * jnp.einsum('bhtd,bhsd->bhts', q, k)
    e_a = jnp.exp(a - m_t)
    num = jnp.einsum('bhts,bhsd->bhtd', w_mat, v) + e_a[..., None] * jnp.einsum('bhtk,bhkv->bhtv', q, c_state)
    den = jnp.sum(w_mat, axis=-1) + e_a * jnp.einsum('bhtk,bhk->bht', q, n_state)
    h = num / jnp.maximum(jnp.abs(den), jnp.exp(-m_t))[..., None]
    w_end = g[..., None] - b + li
    m_new = jnp.maximum(m_state + g, jnp.max(w_end, axis=-1))
    decay = jnp.exp(m_state + g - m_new)
    e_w = jnp.exp(w_end - m_new[..., None])
    c_new = decay[..., None, None] * c_state + jnp.einsum('bhs,bhsk,bhsv->bhkv', e_w, k, v)
    n_new = decay[..., None] * n_state + jnp.einsum('bhs,bhsk->bhk', e_w, k)
    return (c_new, n_new, m_new), h


def mlstm_direction(q, k, v, i_pre, f_pre):
    bsz, nh, s, dh = q.shape
    nc = s // CHUNK
    q = q.astype(jnp.float32)
    k = k.astype(jnp.float32) * (dh ** -0.5)
    v = v.astype(jnp.float32)
    li = i_pre.astype(jnp.float32)
    lf = jax.nn.log_sigmoid(f_pre.astype(jnp.float32))

    def chunks(t):
        return jnp.moveaxis(t.reshape((bsz, nh, nc, CHUNK) + t.shape[3:]), 2, 0)

    init = (jnp.zeros((bsz, nh, dh, dh), jnp.float32),
            jnp.zeros((bsz, nh, dh), jnp.float32),
            jnp.zeros((bsz, nh), jnp.float32))
    _, hs = lax.scan(_mlstm_chunk_step, init, (chunks(q), chunks(k), chunks(v), chunks(li), chunks(lf)))
    return jnp.moveaxis(hs, 0, 2).reshape(bsz, nh, s, dh)


def token_mixer(h, w_in, b_gates, conv_w, conv_b, head_norm_g, w_four, w_out):
    bsz, s, _ = h.shape
    m_w, nh, dh = MLSTM_WIDTH, N_MLSTM_HEADS, MLSTM_HEAD_DIM
    proj = h @ w_in
    qk = jax.nn.silu(centred_conv(proj[..., :2 * m_w], conv_w, conv_b))
    v = proj[..., 2 * m_w:3 * m_w]
    o_pre = proj[..., 3 * m_w:4 * m_w]
    gates = (proj[..., 4 * m_w:4 * m_w + N_GATES] + b_gates).astype(jnp.float32)
    xf = proj[..., 4 * m_w + N_GATES:]

    def heads(t):
        return t.reshape(bsz, s, nh, dh).transpose(0, 2, 1, 3)

    q, k, vh = heads(qk[..., :m_w]), heads(qk[..., m_w:]), heads(v)
    gates = gates.transpose(0, 2, 1)
    i_f = gates[:, 0 * nh:1 * nh]
    f_f = gates[:, 1 * nh:2 * nh]
    i_b = gates[:, 2 * nh:3 * nh]
    f_b = gates[:, 3 * nh:4 * nh]
    h_fwd = mlstm_direction(q, k, vh, i_f, f_f)
    flip = lambda t: jnp.flip(t, axis=2)
    h_bwd = flip(mlstm_direction(flip(q), flip(k), flip(vh), flip(i_b), flip(f_b)))
    hm = h_fwd + h_bwd
    hm = hm * lax.rsqrt(jnp.mean(hm * hm, axis=-1, keepdims=True) + EPS)
    hm = hm.transpose(0, 2, 1, 3).reshape(bsz, s, m_w)
    hm = (hm * head_norm_g.astype(jnp.float32) * jax.nn.sigmoid(o_pre.astype(jnp.float32))).astype(h.dtype)

    xg = xf.astype(jnp.float32).reshape(bsz, s, N_FOURIER_GROUPS, FOURIER_GROUP_DIM)
    spec = jnp.fft.fft2(xg, axes=(1, 3), norm='ortho').real
    yf = jnp.einsum('bsgc,gce->bsge', spec, w_four.astype(jnp.float32))
    yf = yf.reshape(bsz, s, FOURIER_WIDTH).astype(h.dtype)
    return jnp.concatenate([hm, yf], axis=-1) @ w_out


def moe(h, w_router, b_router, w_gate_up, b_gate_up, w_down, b_down):
    bsz, s, d = h.shape
    t = h.reshape(bsz * s, d)
    n_tok = t.shape[0]
    n_assign = n_tok * TOP_K
    logits = (t @ w_router + b_router).astype(jnp.float32)
    top_val, top_idx = lax.top_k(logits, TOP_K)
    gate_w = jax.nn.softmax(top_val, axis=-1)
    flat_e = top_idx.reshape(-1)
    flat_tok = jnp.repeat(jnp.arange(n_tok, dtype=jnp.int32), TOP_K)
    flat_g = gate_w.reshape(-1)
    order = jnp.argsort(flat_e)
    sorted_e = flat_e[order]
    counts = jnp.bincount(flat_e, length=N_EXPERTS)
    offsets = jnp.cumsum(counts) - counts
    p_counts = (counts + MOE_BLOCK - 1) // MOE_BLOCK * MOE_BLOCK
    p_ends = jnp.cumsum(p_counts)
    p_offsets = p_ends - p_counts
    rank = jnp.arange(n_assign, dtype=jnp.int32) - offsets[sorted_e]
    dest = p_offsets[sorted_e] + rank
    n_blocks = -(-n_assign // MOE_BLOCK) + N_EXPERTS
    n_rows = n_blocks * MOE_BLOCK
    row_tok = jnp.zeros((n_rows,), jnp.int32).at[dest].set(flat_tok[order])
    row_g = jnp.zeros((n_rows,), jnp.float32).at[dest].set(flat_g[order])
    block_start = jnp.arange(n_blocks, dtype=jnp.int32) * MOE_BLOCK
    block_e = jnp.clip(jnp.searchsorted(p_ends, block_start, side='right'), 0, N_EXPERTS - 1)

    def expert_block(args):
        tok, gw, e = args
        xb = t[tok]
        gu = xb @ w_gate_up[e] + b_gate_up[e]
        gate = jnp.minimum(gu[:, :D_FF], SWIGLU_LIMIT)
        up = jnp.clip(gu[:, D_FF:], -SWIGLU_LIMIT, SWIGLU_LIMIT)
        act = (up + 1.0) * (gate * jax.nn.sigmoid(SWIGLU_ALPHA * gate))
        y = act @ w_down[e] + b_down[e]
        return (y * gw[:, None]).astype(t.dtype)

    ys = lax.map(expert_block, (row_tok.reshape(n_blocks, MOE_BLOCK), row_g.reshape(n_blocks, MOE_BLOCK), block_e))
    out = jnp.zeros((n_tok, d), t.dtype).at[row_tok].add(ys.reshape(n_rows, d))
    return out.reshape(bsz, s, d)


def encoder_layer(x, c, ln1_g, ln2_g, w_ada, b_ada, w_in, b_gates, conv_w, conv_b, head_norm_g,
                  w_four, w_out, w_router, b_router, w_gate_up, b_gate_up, w_down, b_down):
    ada = (jax.nn.silu(c) @ w_ada + b_ada)[:, None, :]
    sh1, sc1, g1, sh2, sc2, g2 = jnp.split(ada, 6, axis=-1)
    h = rmsnorm(x, ln1_g) * (1.0 + sc1) + sh1
    x = x + g1 * token_mixer(h, w_in, b_gates, conv_w, conv_b, head_norm_g, w_four, w_out)
    h = rmsnorm(x, ln2_g) * (1.0 + sc2) + sh2
    x = x + g2 * moe(h, w_router, b_router, w_gate_up, b_gate_up, w_down, b_down)
    return x


def setup_inputs(seed: int = 0) -> dict:
    key = jax.random.key(seed)
    ks = jax.random.split(key, 24)
    d, hh = D_MODEL, N_MLSTM_HEADS

    def nrm(k, shape, scale):
        return jax.random.normal(k, shape, jnp.float32) * scale

    gate_base = jnp.concatenate([jnp.zeros((hh,), jnp.float32), jnp.linspace(3.0, 6.0, hh, dtype=jnp.float32),
                                 jnp.zeros((hh,), jnp.float32), jnp.linspace(3.0, 6.0, hh, dtype=jnp.float32)])
    return {
        'x_prompt': nrm(ks[0], (BATCH, SEQ, d), 1.0),
        'x_sample': nrm(ks[1], (DEC_BATCH, DEC_SEQ, d), 1.0),
        'c_prompt': nrm(ks[2], (BATCH, d), 1.0),
        'c_sample': nrm(ks[3], (DEC_BATCH, d), 1.0),
        'ln1_g': 1.0 + nrm(ks[4], (DEPTH, d), 0.02),
        'ln2_g': 1.0 + nrm(ks[5], (DEPTH, d), 0.02),
        'w_ada': nrm(ks[6], (DEPTH, d, 6 * d), 0.5 * d ** -0.5),
        'b_ada': nrm(ks[7], (DEPTH, 6 * d), 0.01),
        'w_in': nrm(ks[8], (DEPTH, d, PROJ_WIDTH), d ** -0.5),
        'b_gates': gate_base + nrm(ks[9], (DEPTH, N_GATES), 0.01),
        'conv_w': nrm(ks[10], (DEPTH, CONV_WIDTH, 2 * MLSTM_WIDTH), CONV_WIDTH ** -0.5),
        'conv_b': nrm(ks[11], (DEPTH, 2 * MLSTM_WIDTH), 0.01),
        'head_norm_g': 1.0 + nrm(ks[12], (DEPTH, MLSTM_WIDTH), 0.02),
        'w_four': nrm(ks[13], (DEPTH, N_FOURIER_GROUPS, FOURIER_GROUP_DIM, FOURIER_GROUP_DIM), FOURIER_GROUP_DIM ** -0.5),
        'w_out': nrm(ks[14], (DEPTH, MIX_WIDTH, d), MIX_WIDTH ** -0.5),
        'w_router': nrm(ks[15], (DEPTH, d, N_EXPERTS), d ** -0.5),
        'b_router': nrm(ks[16], (DEPTH, N_EXPERTS), 0.01),
        'w_gate_up': nrm(ks[17], (DEPTH, N_EXPERTS, d, 2 * D_FF), d ** -0.5),
        'b_gate_up': nrm(ks[18], (DEPTH, N_EXPERTS, 2 * D_FF), 0.01),
        'w_down': nrm(ks[19], (DEPTH, N_EXPERTS, D_FF, d), D_FF ** -0.5),
        'b_down': nrm(ks[20], (DEPTH, N_EXPERTS, d), 0.01),
        'final_g': 1.0 + nrm(ks[21], (d,), 0.02),
    }


def reference(x_prompt, x_sample, c_prompt, c_sample, ln1_g, ln2_g, w_ada, b_ada, w_in, b_gates, conv_w,
              conv_b, head_norm_g, w_four, w_out, w_router, b_router, w_gate_up, b_gate_up, w_down, b_down,
              final_g):
    xp = x_prompt
    xs = x_sample
    for l in range(DEPTH):
        lp = (ln1_g[l], ln2_g[l], w_ada[l], b_ada[l], w_in[l], b_gates[l], conv_w[l], conv_b[l],
              head_norm_g[l], w_four[l], w_out[l], w_router[l], b_router[l], w_gate_up[l], b_gate_up[l],
              w_down[l], b_down[l])
        xp = encoder_layer(xp, c_prompt, *lp)
        xs = encoder_layer(xs, c_sample, *lp)
    y_prompt = rmsnorm(xp, final_g)
    y_sample = rmsnorm(xs, final_g)
    return (y_prompt, y_sample)
```

```python
import functools
import math

import numpy as np
import jax
import jax.numpy as jnp
from jax import lax
from jax.experimental import pallas as pl
from jax.experimental.pallas import tpu as pltpu

F32 = jnp.float32
BF16 = jnp.bfloat16

N_HEADS = 4
N_GROUPS = 4
N_GATES = 4 * N_HEADS
CHUNK = 128
TOP_K = 4
SWIGLU_LIMIT = 7.0
SWIGLU_ALPHA = 1.702
EPS = 1e-6
LANES = 128
SUBLANES = 8
DFT_N2 = 128
MOE_BM = 512
VMEM_LIMIT = 56 * 1024 * 1024


def _params(*sem):
    return pltpu.CompilerParams(dimension_semantics=sem, vmem_limit_bytes=VMEM_LIMIT)


def _tile(n, pref):
    t = min(n, pref)
    while n % t:
        t //= 2
    return t


def _ada_kernel(c_ref, w_ref, b_ref, o_ref):
    c = c_ref[...]
    s = (c * jax.nn.sigmoid(c)).astype(BF16)
    o_ref[...] = jnp.dot(s, w_ref[...].astype(BF16), preferred_element_type=F32) + b_ref[...]


def _ada(c, w_ada, b_ada):
    bsz, d = c.shape
    bp = -(-bsz // SUBLANES) * SUBLANES
    cp = jnp.pad(c, ((0, bp - bsz), (0, 0)))
    n_out = w_ada.shape[1]
    tn = _tile(n_out, 1024)
    out = pl.pallas_call(
        _ada_kernel,
        grid=(n_out // tn,),
        in_specs=[pl.BlockSpec((bp, d), lambda j: (0, 0)),
                  pl.BlockSpec((d, tn), lambda j: (0, j)),
                  pl.BlockSpec((1, tn), lambda j: (0, j))],
        out_specs=pl.BlockSpec((bp, tn), lambda j: (0, j)),
        out_shape=jax.ShapeDtypeStruct((bp, n_out), F32),
        compiler_params=_params("arbitrary"),
        name="ada",
    )(cp, w_ada, b_ada.reshape(1, n_out))
    return out[:bsz]


def _rms(x, g):
    return (x * lax.rsqrt(jnp.mean(x * x, axis=-1, keepdims=True) + EPS)) * g


def _inproj_kernel(x_ref, g_ref, sc_ref, sh_ref, w_ref, wg_ref, bg_ref, o_ref, og_ref, h_scr):
    @pl.when(pl.program_id(1) == 0)
    def _():
        h = _rms(x_ref[...], g_ref[...]) * (1.0 + sc_ref[0]) + sh_ref[0]
        hb = h.astype(BF16)
        h_scr[...] = hb
        og_ref[...] = jnp.dot(hb, wg_ref[...], preferred_element_type=F32) + bg_ref[...]

    o_ref[...] = jnp.dot(h_scr[...], w_ref[...], preferred_element_type=F32)


def _inproj(x2, ln_g, sc, sh, w_main, w_gates, b_gates, seq):
    n, d = x2.shape
    p = w_main.shape[1]
    tm = _tile(seq, 1024)
    tn = _tile(p, 1024)
    tps = seq // tm
    return pl.pallas_call(
        _inproj_kernel,
        grid=(n // tm, p // tn),
        in_specs=[pl.BlockSpec((tm, d), lambda i, j: (i, 0)),
                  pl.BlockSpec((1, d), lambda i, j: (0, 0)),
                  pl.BlockSpec((1, 1, d), lambda i, j: (i // tps, 0, 0)),
                  pl.BlockSpec((1, 1, d), lambda i, j: (i // tps, 0, 0)),
                  pl.BlockSpec((d, tn), lambda i, j: (0, j)),
                  pl.BlockSpec((d, LANES), lambda i, j: (0, 0)),
                  pl.BlockSpec((1, LANES), lambda i, j: (0, 0))],
        out_specs=[pl.BlockSpec((tm, tn), lambda i, j: (i, j)),
                   pl.BlockSpec((tm, LANES), lambda i, j: (i, 0))],
        out_shape=[jax.ShapeDtypeStruct((n, p), F32),
                   jax.ShapeDtypeStruct((n, LANES), F32)],
        scratch_shapes=[pltpu.VMEM((tm, d), BF16)],
        compiler_params=_params("arbitrary", "arbitrary"),
        name="inproj",
    )(x2, ln_g.reshape(1, d), sc, sh, w_main, w_gates, b_gates)


def _conv_kernel(x_ref, prev_ref, next_ref, w_ref, b_ref, o_ref, *, tps, k_scale):
    it = pl.program_id(0) % tps
    w = w_ref[...]
    b = b_ref[...]
    taps = w.shape[0]
    half = taps // 2
    scale = jnp.where(pl.program_id(1) == 1, k_scale, 1.0).astype(F32)

    def conv_act(z):
        nz = z.shape[0]
        acc = z * w[half:half + 1]
        for j in range(taps):
            if j != half:
                acc = acc + pltpu.roll(z, (half - j) % nz, 0) * w[j:j + 1]
        acc = acc + b
        return (acc * jax.nn.sigmoid(acc)) * scale

    x = x_ref[...]
    tm = x.shape[0]
    o_ref[...] = conv_act(x)
    prev = jnp.where(it == 0, 0.0, prev_ref[...])
    nxt = jnp.where(it == tps - 1, 0.0, next_ref[...])
    top = conv_act(jnp.concatenate([prev, x[0:2 * SUBLANES]], axis=0))
    o_ref[0:SUBLANES, :] = top[SUBLANES:2 * SUBLANES]
    bot = conv_act(jnp.concatenate([x[tm - 2 * SUBLANES:tm], nxt], axis=0))
    o_ref[tm - SUBLANES:tm, :] = bot[SUBLANES:2 * SUBLANES]


def _conv(proj, conv_w, conv_b, seq, mw, k_scale):
    n = proj.shape[0]
    taps = conv_w.shape[0]
    assert taps // 2 <= SUBLANES
    tm = _tile(seq, 512)
    assert tm >= 4 * SUBLANES
    tps = seq // tm
    r8 = tm // SUBLANES
    last8 = n // SUBLANES - 1
    return pl.pallas_call(
        functools.partial(_conv_kernel, tps=tps, k_scale=k_scale),
        grid=(n // tm, 2),
        in_specs=[pl.BlockSpec((tm, mw), lambda i, j: (i, j)),
                  pl.BlockSpec((SUBLANES, mw), lambda i, j: (jnp.maximum(i * r8 - 1, 0), j)),
                  pl.BlockSpec((SUBLANES, mw), lambda i, j: (jnp.minimum((i + 1) * r8, last8), j)),
                  pl.BlockSpec((taps, mw), lambda i, j: (0, j)),
                  pl.BlockSpec((1, mw), lambda i, j: (0, j))],
        out_specs=pl.BlockSpec((tm, mw), lambda i, j: (i, j)),
        out_shape=jax.ShapeDtypeStruct((n, 2 * mw), F32),
        compiler_params=_params("arbitrary", "arbitrary"),
        name="conv",
    )(proj, proj, proj, conv_w, conv_b.reshape(1, 2 * mw))


def _log_sigmoid(x):
    return jnp.minimum(x, 0.0) - jnp.log1p(jnp.exp(-jnp.abs(x)))


def _split3(x):
    hi = x.astype(BF16)
    r = x - hi.astype(F32)
    mid = r.astype(BF16)
    lo = (r - mid.astype(F32)).astype(BF16)
    return hi, mid, lo


def _mlstm_chunk(q, k, v, li_c, b_c, li_r, b_r, g, c_ref, n_ref, m_ref, idx, reverse):
    length = q.shape[0]
    m = m_ref[idx][:, 0:1]
    c_state = c_ref[idx]
    n_state = n_ref[idx]
    qb = q.astype(BF16)
    kb = k.astype(BF16)
    vb = v.astype(BF16)
    sqk = lax.dot_general(qb, kb, (((1,), (1,)), ((), ())), preferred_element_type=F32)
    row = lax.broadcasted_iota(jnp.int32, (length, length), 0)
    col = lax.broadcasted_iota(jnp.int32, (length, length), 1)
    mask = (col >= row) if reverse else (col <= row)
    d = jnp.where(mask, b_c - b_r + li_r, -jnp.inf)
    a = b_c + m
    m_t = jnp.maximum(a, jnp.max(d, axis=1, keepdims=True))
    wm = jnp.exp(d - m_t) * sqk
    e_a = jnp.exp(a - m_t)
    num = (jnp.dot(wm.astype(BF16), vb, preferred_element_type=F32)
           + e_a * jnp.dot(qb, c_state.astype(BF16), preferred_element_type=F32))
    den = (jnp.sum(wm, axis=1, keepdims=True)
           + e_a * jnp.sum(q * n_state, axis=1, keepdims=True))
    h = num / jnp.maximum(jnp.abs(den), jnp.exp(-m_t))
    w_end = g - b_c + li_c
    m_new = jnp.maximum(m + g, jnp.max(w_end, axis=0, keepdims=True))
    decay = jnp.exp(m + g - m_new)
    kw = k * jnp.exp(w_end - m_new)
    c_ref[idx] = decay * c_state + lax.dot_general(
        kw.astype(BF16), vb, (((0,), (0,)), ((), ())), preferred_element_type=F32)
    n_ref[idx] = decay * n_state + jnp.sum(kw, axis=0, keepdims=True)
    m_ref[idx] = jnp.broadcast_to(m_new, (1, LANES))
    return h


def _mlstm_kernel(qf_ref, kf_ref, vf_ref, gcf_ref, grf_ref,
                  qb_ref, kb_ref, vb_ref, gcb_ref, grb_ref,
                  l3_ref, u3_ref, l3t_ref, u3t_ref,
                  hf_ref, hb_ref, c_scr, n_scr, m_scr):
    @pl.when(pl.program_id(1) == 0)
    def _():
        c_scr[...] = jnp.zeros_like(c_scr)
        n_scr[...] = jnp.zeros_like(n_scr)
        m_scr[...] = jnp.zeros_like(m_scr)

    dh = qf_ref.shape[1] // N_HEADS
    length = qf_ref.shape[0]
    dirs = (
        (False, qf_ref, kf_ref, vf_ref, gcf_ref, grf_ref, l3_ref, u3t_ref, hf_ref, 0),
        (True, qb_ref, kb_ref, vb_ref, gcb_ref, grb_ref, u3_ref, l3t_ref, hb_ref, 2 * N_HEADS),
    )
    for reverse, q_ref, k_ref, v_ref, gc_ref, gr_ref, tri_c, tri_r, h_ref, col0 in dirs:
        gc = gc_ref[...]
        gr = gr_ref[...]
        b_cols = jnp.dot(tri_c[...], jnp.concatenate(_split3(_log_sigmoid(gc)), axis=0),
                         preferred_element_type=F32)
        b_rows = jnp.dot(jnp.concatenate(_split3(_log_sigmoid(gr)), axis=1), tri_r[...],
                         preferred_element_type=F32)
        for hd in range(N_HEADS):
            ci = col0 + hd
            cf = col0 + N_HEADS + hd
            b_c = b_cols[:, cf:cf + 1]
            g = b_c[0:1, :] if reverse else b_c[length - 1:length, :]
            sl = slice(hd * dh, (hd + 1) * dh)
            h = _mlstm_chunk(q_ref[:, sl], k_ref[:, sl], v_ref[:, sl],
                             gc[:, ci:ci + 1], b_c, gr[ci:ci + 1, :], b_rows[cf:cf + 1, :], g,
                             c_scr, n_scr, m_scr, (1 if reverse else 0) * N_HEADS + hd, reverse)
            h_ref[:, sl] = h


def _tri_consts(length):
    lower = np.tril(np.ones((length, length), np.float32))
    upper = lower.T
    l3 = np.concatenate([lower] * 3, axis=1)
    u3 = np.concatenate([upper] * 3, axis=1)
    l3t = np.concatenate([lower] * 3, axis=0)
    u3t = np.concatenate([upper] * 3, axis=0)
    return tuple(jnp.asarray(a, BF16) for a in (l3, u3, l3t, u3t))


def _mlstm(qk, proj, gates, gates_t, bsz, seq, mw):
    n = qk.shape[0]
    nc = seq // CHUNK
    dh = mw // N_HEADS
    fwd = lambda b, c: b * nc + c
    bwd = lambda b, c: b * nc + (nc - 1 - c)
    cst = lambda b, c: (0, 0)
    in_specs = []
    for pos in (fwd, bwd):
        in_specs += [
            pl.BlockSpec((CHUNK, mw), lambda b, c, pos=pos: (pos(b, c), 0)),
            pl.BlockSpec((CHUNK, mw), lambda b, c, pos=pos: (pos(b, c), 1)),
            pl.BlockSpec((CHUNK, mw), lambda b, c, pos=pos: (pos(b, c), 2)),
            pl.BlockSpec((CHUNK, LANES), lambda b, c, pos=pos: (pos(b, c), 0)),
            pl.BlockSpec((N_GATES, CHUNK), lambda b, c, pos=pos: (b, pos(0, c))),
        ]
    in_specs += [pl.BlockSpec((CHUNK, 3 * CHUNK), cst), pl.BlockSpec((CHUNK, 3 * CHUNK), cst),
                 pl.BlockSpec((3 * CHUNK, CHUNK), cst), pl.BlockSpec((3 * CHUNK, CHUNK), cst)]
    return pl.pallas_call(
        _mlstm_kernel,
        grid=(bsz, nc),
        in_specs=in_specs,
        out_specs=[pl.BlockSpec((CHUNK, mw), lambda b, c: (fwd(b, c), 0)),
                   pl.BlockSpec((CHUNK, mw), lambda b, c: (bwd(b, c), 0))],
        out_shape=[jax.ShapeDtypeStruct((n, mw), F32), jax.ShapeDtypeStruct((n, mw), F32)],
        scratch_shapes=[pltpu.VMEM((2 * N_HEADS, dh, dh), F32),
                        pltpu.VMEM((2 * N_HEADS, 1, dh), F32),
                        pltpu.VMEM((2 * N_HEADS, 1, LANES), F32)],
        compiler_params=_params("arbitrary", "arbitrary"),
        name="mlstm",
    )(qk, qk, proj, gates, gates_t, qk, qk, proj, gates, gates_t, *_tri_consts(CHUNK))


def _dft_a_kernel(x_ref, ga_ref, tc_ref, ts_ref, o_ref, *, nb, width):
    r = x_ref.shape[0]
    z = jnp.dot(ga_ref[...], x_ref[...].astype(BF16), preferred_element_type=F32)
    reps = width // LANES
    for l in range(nb):
        zc = z[0:r, l * width:(l + 1) * width]
        zs = z[r:2 * r, l * width:(l + 1) * width]
        tc = jnp.concatenate([tc_ref[l]] * reps, axis=1)
        ts = jnp.concatenate([ts_ref[l]] * reps, axis=1)
        o_ref[0, :, l * width:(l + 1) * width] = zc * tc - zs * ts
        o_ref[1, :, l * width:(l + 1) * width] = zc * ts + zs * tc


def _dft_b_kernel(z_ref, gb_ref, o_ref):
    o_ref[0] = jnp.dot(gb_ref[...], z_ref[0].astype(BF16), preferred_element_type=F32)


def _dft_consts(bsz, n1, n2, seq):
    k1 = np.arange(n1)
    ang1 = 2.0 * np.pi * np.outer(k1, k1) / n1
    eye = np.eye(bsz)
    ga = np.concatenate([np.kron(eye, np.cos(ang1)), np.kron(eye, np.sin(ang1))], axis=0) / math.sqrt(n1)
    k2 = np.arange(n2)
    ang2 = 2.0 * np.pi * np.outer(k2, k2) / n2
    c2, s2 = np.cos(ang2), np.sin(ang2)
    gb = np.block([[c2, -s2], [s2, c2]]) / math.sqrt(n2)
    return jnp.asarray(ga, BF16), jnp.asarray(gb, BF16)


def _position_dft(xf, bsz, seq):
    n, width = xf.shape
    n2 = DFT_N2
    n1 = seq // n2
    r = bsz * n1
    ga, gb = _dft_consts(bsz, n1, n2, seq)
    k1 = jnp.tile(jnp.arange(n1, dtype=jnp.int32), bsz)
    ang = (2.0 * math.pi / seq) * ((jnp.arange(n2, dtype=jnp.int32)[:, None] * k1[None, :]) % seq).astype(F32)
    tc = jnp.broadcast_to(jnp.cos(ang)[:, :, None], (n2, r, LANES))
    ts = jnp.broadcast_to(jnp.sin(ang)[:, :, None], (n2, r, LANES))
    nb = 4
    x2 = xf.reshape(r, n2 * width)
    z = pl.pallas_call(
        functools.partial(_dft_a_kernel, nb=nb, width=width),
        grid=(n2 // nb,),
        in_specs=[pl.BlockSpec((r, nb * width), lambda j: (0, j)),
                  pl.BlockSpec((2 * r, r), lambda j: (0, 0)),
                  pl.BlockSpec((nb, r, LANES), lambda j: (j, 0, 0)),
                  pl.BlockSpec((nb, r, LANES), lambda j: (j, 0, 0))],
        out_specs=pl.BlockSpec((2, r, nb * width), lambda j: (0, 0, j)),
        out_shape=jax.ShapeDtypeStruct((2, r, n2 * width), F32),
        compiler_params=_params("arbitrary"),
        name="dft_a",
    )(x2, ga, tc, ts)
    zt = z.reshape(2, bsz, n1, n2, width).transpose(1, 0, 3, 2, 4).reshape(bsz, 2 * n2, n1 * width)
    tn = _tile(n1 * width, 8192)
    p = pl.pallas_call(
        _dft_b_kernel,
        grid=(bsz, (n1 * width) // tn),
        in_specs=[pl.BlockSpec((1, 2 * n2, tn), lambda b, j: (b, 0, j)),
                  pl.BlockSpec((2 * n2, 2 * n2), lambda b, j: (0, 0))],
        out_specs=pl.BlockSpec((1, 2 * n2, tn), lambda b, j: (b, 0, j)),
        out_shape=jax.ShapeDtypeStruct((bsz, 2 * n2, n1 * width), F32),
        compiler_params=_params("arbitrary", "arbitrary"),
        name="dft_b",
    )(zt, gb)
    p = p.reshape(bsz, 2, n2 * n1, width)
    return p[:, 0].reshape(n, width), p[:, 1].reshape(n, width)


def _mix_kernel(hf_ref, hb_ref, op_ref, pc_ref, ps_ref, x_ref, g1_ref, sc_ref, sh_ref,
                hng_ref, ln_ref, wout_ref, mc_ref, wf_ref, wrh_ref, wrl_ref, br_ref, ltri_ref,
                x1_ref, h2_ref, idx_ref, gw_ref, rank_ref, cnt_ref, base_scr):
    @pl.when(pl.program_id(0) == 0)
    def _():
        base_scr[...] = jnp.zeros_like(base_scr)

    tm, mw = hf_ref.shape
    dh = mw // N_HEADS
    hm = hf_ref[...] + hb_ref[...]
    parts = []
    for hd in range(N_HEADS):
        seg = hm[:, hd * dh:(hd + 1) * dh]
        parts.append(seg * lax.rsqrt(jnp.mean(seg * seg, axis=-1, keepdims=True) + EPS))
    hm = jnp.concatenate(parts, axis=1) * hng_ref[...] * jax.nn.sigmoid(op_ref[...])
    cg = pc_ref.shape[1] // N_GROUPS
    pc = pc_ref[...]
    ps = ps_ref[...]
    for gi in range(N_GROUPS):
        sl = slice(gi * cg, (gi + 1) * cg)
        spec = jnp.dot(jnp.concatenate([pc[:, sl], ps[:, sl]], axis=1).astype(BF16), mc_ref[...],
                       preferred_element_type=F32)
        parts.append(jnp.dot(spec.astype(BF16), wf_ref[gi].astype(BF16), preferred_element_type=F32))
    cat = jnp.concatenate([hm] + parts[N_HEADS:], axis=1).astype(BF16)
    mix = jnp.dot(cat, wout_ref[...], preferred_element_type=F32)
    x1 = x_ref[...] + g1_ref[0] * mix
    x1_ref[...] = x1
    h2 = _rms(x1, ln_ref[...]) * (1.0 + sc_ref[0]) + sh_ref[0]
    h2_ref[...] = h2

    hh = h2.astype(BF16)
    hl = (h2 - hh.astype(F32)).astype(BF16)
    wrh = wrh_ref[...]
    logits = (jnp.dot(hh, wrh, preferred_element_type=F32)
              + jnp.dot(hl, wrh, preferred_element_type=F32)
              + jnp.dot(hh, wrl_ref[...], preferred_element_type=F32)) + br_ref[...]
    n_exp = logits.shape[1]
    lane = lax.broadcasted_iota(jnp.int32, (tm, n_exp), 1).astype(F32)
    lane_o = lax.broadcasted_iota(jnp.int32, (tm, LANES), 1)
    base = base_scr[...]
    ltri = ltri_ref[...]
    idx_out = jnp.zeros((tm, LANES), F32)
    val_out = jnp.zeros((tm, LANES), F32)
    rank_out = jnp.zeros((tm, LANES), F32)
    top0 = None
    denom = jnp.zeros((tm, 1), F32)
    work = logits
    for k in range(TOP_K):
        mx = jnp.max(work, axis=1, keepdims=True)
        ik = jnp.min(jnp.where(work == mx, lane, float(n_exp)), axis=1, keepdims=True)
        hit = lane == ik
        work = jnp.where(hit, -jnp.inf, work)
        if top0 is None:
            top0 = mx
        ek = jnp.exp(mx - top0)
        denom = denom + ek
        onehot = hit.astype(F32)
        before = jnp.dot(ltri, onehot.astype(BF16), preferred_element_type=F32)
        rk = jnp.sum(jnp.where(hit, base + before, 0.0), axis=1, keepdims=True)
        base = base + jnp.sum(onehot, axis=0, keepdims=True)
        idx_out = jnp.where(lane_o == k, ik, idx_out)
        val_out = jnp.where(lane_o == k, ek, val_out)
        rank_out = jnp.where(lane_o == k, rk, rank_out)
    idx_ref[...] = idx_out.astype(jnp.int32)
    gw_ref[...] = val_out / denom
    rank_ref[...] = rank_out
    base_scr[...] = base
    cnt_ref[...] = base


def _mix(hf, hb, proj, pc, ps, x2, g1, sc2, sh2, hng, ln2_g, w_out, mc, w_four, wr_hi, wr_lo, b_router, seq):
    n, d = x2.shape
    mw = hf.shape[1]
    fw = pc.shape[1]
    n_exp = wr_hi.shape[1]
    cg = fw // N_GROUPS
    tm = _tile(seq, 256)
    tps = seq // tm
    ltri = jnp.asarray(np.tril(np.ones((tm, tm), np.float32), -1), BF16)
    row = lambda i: (i, 0)
    cst = lambda i: (0, 0)
    bat = lambda i: (i // tps, 0, 0)
    return pl.pallas_call(
        _mix_kernel,
        grid=(n // tm,),
        in_specs=[pl.BlockSpec((tm, mw), row), pl.BlockSpec((tm, mw), row),
                  pl.BlockSpec((tm, mw), lambda i: (i, 3)),
                  pl.BlockSpec((tm, fw), row), pl.BlockSpec((tm, fw), row),
                  pl.BlockSpec((tm, d), row),
                  pl.BlockSpec((1, 1, d), bat), pl.BlockSpec((1, 1, d), bat), pl.BlockSpec((1, 1, d), bat),
                  pl.BlockSpec((1, mw), cst), pl.BlockSpec((1, d), cst),
                  pl.BlockSpec((mw + fw, d), cst),
                  pl.BlockSpec((2 * cg, cg), cst),
                  pl.BlockSpec((N_GROUPS, cg, cg), lambda i: (0, 0, 0)),
                  pl.BlockSpec((d, n_exp), cst), pl.BlockSpec((d, n_exp), cst),
                  pl.BlockSpec((1, n_exp), cst),
                  pl.BlockSpec((tm, tm), cst)],
        out_specs=[pl.BlockSpec((tm, d), row), pl.BlockSpec((tm, d), row),
                   pl.BlockSpec((tm, LANES), row), pl.BlockSpec((tm, LANES), row),
                   pl.BlockSpec((tm, LANES), row), pl.BlockSpec((1, n_exp), cst)],
        out_shape=[jax.ShapeDtypeStruct((n, d), F32), jax.ShapeDtypeStruct((n, d), F32),
                   jax.ShapeDtypeStruct((n, LANES), jnp.int32), jax.ShapeDtypeStruct((n, LANES), F32),
                   jax.ShapeDtypeStruct((n, LANES), F32), jax.ShapeDtypeStruct((1, n_exp), F32)],
        scratch_shapes=[pltpu.VMEM((1, n_exp), F32)],
        compiler_params=_params("arbitrary"),
        name="mix",
    )(hf, hb, proj, pc, ps, x2, g1, sc2, sh2, hng.reshape(1, mw), ln2_g.reshape(1, d),
      w_out, mc, w_four, wr_hi, wr_lo, b_router.reshape(1, n_exp), ltri)


def _row_copy(src, src_row, dst, dst_row, sem):
    return pltpu.make_async_copy(src.at[pl.ds(src_row, 1), :], dst.at[pl.ds(dst_row, 1), :], sem)


def _dispatch_kernel(dest_ref, h_ref, xs_in_ref, xs_ref, sem):
    del xs_in_ref
    tm = h_ref.shape[0]

    def issue(t, carry):
        for k in range(TOP_K):
            _row_copy(h_ref, t, xs_ref, dest_ref[t * TOP_K + k], sem).start()
        return carry

    def drain(t, carry):
        for k in range(TOP_K):
            _row_copy(h_ref, t, xs_ref, dest_ref[t * TOP_K + k], sem).wait()
        return carry

    lax.fori_loop(0, tm, issue, 0)
    lax.fori_loop(0, tm, drain, 0)


def _dispatch(h2, dest, xs_init):
    n, d = h2.shape
    tm = _tile(n, 256)
    return pl.pallas_call(
        _dispatch_kernel,
        grid=(n // tm,),
        in_specs=[pl.BlockSpec((tm * TOP_K,), lambda i: (i,), memory_space=pltpu.SMEM),
                  pl.BlockSpec((tm, d), lambda i: (i, 0)),
                  pl.BlockSpec(memory_space=pl.ANY)],
        out_specs=pl.BlockSpec(memory_space=pl.ANY),
        out_shape=jax.ShapeDtypeStruct(xs_init.shape, F32),
        scratch_shapes=[pltpu.SemaphoreType.DMA(())],
        input_output_aliases={2: 0},
        compiler_params=_params("arbitrary"),
        name="dispatch",
    )(dest, h2, xs_init)


def _combine_kernel(dest_ref, x1_ref, gw_ref, g2_ref, fg_ref, ys_ref, o_ref, buf, sem):
    tm = x1_ref.shape[0]

    def issue(t, carry):
        for k in range(TOP_K):
            _row_copy(ys_ref, dest_ref[t * TOP_K + k], buf.at[k], t, sem).start()
        return carry

    def drain(t, carry):
        for k in range(TOP_K):
            _row_copy(ys_ref, dest_ref[t * TOP_K + k], buf.at[k], t, sem).wait()
        return carry

    lax.fori_loop(0, tm, issue, 0)
    lax.fori_loop(0, tm, drain, 0)
    gw = gw_ref[...]
    y = gw[:, 0:1] * buf[0]
    for k in range(1, TOP_K):
        y = y + gw[:, k:k + 1] * buf[k]
    x2 = x1_ref[...] + g2_ref[0] * y
    o_ref[...] = _rms(x2, fg_ref[...])


def _combine(x1, gw, g2, final_g, ys, dest, seq):
    n, d = x1.shape
    tm = _tile(seq, 256)
    tps = seq // tm
    return pl.pallas_call(
        _combine_kernel,
        grid=(n // tm,),
        in_specs=[pl.BlockSpec((tm * TOP_K,), lambda i: (i,), memory_space=pltpu.SMEM),
                  pl.BlockSpec((tm, d), lambda i: (i, 0)),
                  pl.BlockSpec((tm, LANES), lambda i: (i, 0)),
                  pl.BlockSpec((1, 1, d), lambda i: (i // tps, 0, 0)),
                  pl.BlockSpec((1, d), lambda i: (0, 0)),
                  pl.BlockSpec(memory_space=pl.ANY)],
        out_specs=pl.BlockSpec((tm, d), lambda i: (i, 0)),
        out_shape=jax.ShapeDtypeStruct((n, d), F32),
        scratch_shapes=[pltpu.VMEM((TOP_K, tm, d), F32), pltpu.SemaphoreType.DMA(())],
        compiler_params=_params("arbitrary"),
        name="combine",
    )(dest, x1, gw, g2, final_g.reshape(1, d), ys)


def _moe_kernel(be_ref, bv_ref, x_ref, wg_ref, wu_ref, wd_ref, bg_ref, bu_ref, bd_ref, o_ref, xb_scr):
    del be_ref
    valid = bv_ref[pl.program_id(0)] > 0

    @pl.when(pl.program_id(1) == 0)
    def _():
        o_ref[...] = jnp.broadcast_to(bd_ref[0], o_ref.shape)

    @pl.when(jnp.logical_and(valid, pl.program_id(1) == 0))
    def _():
        xb_scr[...] = x_ref[...].astype(BF16)

    @pl.when(valid)
    def _():
        xb = xb_scr[...]
        gate = jnp.dot(xb, wg_ref[0], preferred_element_type=F32) + bg_ref[0]
        up = jnp.dot(xb, wu_ref[0], preferred_element_type=F32) + bu_ref[0]
        gate = jnp.minimum(gate, SWIGLU_LIMIT)
        up = jnp.clip(up, -SWIGLU_LIMIT, SWIGLU_LIMIT)
        act = (up + 1.0) * (gate * jax.nn.sigmoid(SWIGLU_ALPHA * gate))
        o_ref[...] += jnp.dot(act.astype(BF16), wd_ref[0], preferred_element_type=F32)


def _moe(xs, block_e, block_valid, w_gu, b_gu, w_dn, b_dn):
    rows, d = xs.shape
    n_exp, d_ff = w_dn.shape[0], w_dn.shape[1]
    fc = _tile(d_ff, 512)
    nf = d_ff // fc
    n_blocks = rows // MOE_BM
    grid_spec = pltpu.PrefetchScalarGridSpec(
        num_scalar_prefetch=2,
        grid=(n_blocks, nf),
        in_specs=[pl.BlockSpec((MOE_BM, d), lambda i, f, be, bv: (i, 0)),
                  pl.BlockSpec((1, d, fc), lambda i, f, be, bv: (be[i], 0, f)),
                  pl.BlockSpec((1, d, fc), lambda i, f, be, bv: (be[i], 0, nf + f)),
                  pl.BlockSpec((1, fc, d), lambda i, f, be, bv: (be[i], f, 0)),
                  pl.BlockSpec((1, 1, fc), lambda i, f, be, bv: (be[i], 0, f)),
                  pl.BlockSpec((1, 1, fc), lambda i, f, be, bv: (be[i], 0, nf + f)),
                  pl.BlockSpec((1, 1, d), lambda i, f, be, bv: (be[i], 0, 0))],
        out_specs=pl.BlockSpec((MOE_BM, d), lambda i, f, be, bv: (i, 0)),
        scratch_shapes=[pltpu.VMEM((MOE_BM, d), BF16)],
    )
    return pl.pallas_call(
        _moe_kernel,
        grid_spec=grid_spec,
        out_shape=jax.ShapeDtypeStruct((rows, d), F32),
        compiler_params=_params("arbitrary", "arbitrary"),
        name="moe",
    )(block_e, block_valid, xs, w_gu, w_gu, w_dn,
      b_gu.reshape(n_exp, 1, 2 * d_ff), b_gu.reshape(n_exp, 1, 2 * d_ff), b_dn.reshape(n_exp, 1, d))


def _encoder_layer(x, ada, wts, final_g):
    (ln1_g, ln2_g, w_main, w_gates, b_gates, conv_w, conv_b, head_norm_g, w_four, w_out_b, mc,
     wr_hi, wr_lo, b_router, w_gu, b_gu, w_dn, b_dn) = wts
    bsz, seq, d = x.shape
    n = bsz * seq
    mw = d // 2
    dh = mw // N_HEADS
    x2 = x.reshape(n, d)
    sh1, sc1, g1, sh2, sc2, g2 = [a.reshape(bsz, 1, d) for a in jnp.split(ada, 6, axis=-1)]

    proj, gates = _inproj(x2, ln1_g, sc1, sh1, w_main, w_gates, b_gates, seq)
    qk = _conv(proj, conv_w, conv_b, seq, mw, dh ** -0.5)
    gates_t = gates[:, :N_GATES].reshape(bsz, seq, N_GATES).transpose(0, 2, 1).reshape(bsz * N_GATES, seq)
    hf, hb = _mlstm(qk, proj, gates, gates_t, bsz, seq, mw)
    pc, ps = _position_dft(proj[:, 4 * mw:], bsz, seq)
    x1, h2, idx, gw, rank, cnt = _mix(hf, hb, proj, pc, ps, x2, g1, sc2, sh2, head_norm_g, ln2_g,
                                      w_out_b, mc, w_four, wr_hi, wr_lo, b_router, seq)

    n_exp = wr_hi.shape[1]
    counts = cnt[0].astype(jnp.int32)
    p_counts = (counts + MOE_BM - 1) // MOE_BM * MOE_BM
    p_ends = jnp.cumsum(p_counts)
    p_off = p_ends - p_counts
    dest = (p_off[idx[:, :TOP_K]] + rank[:, :TOP_K].astype(jnp.int32)).reshape(-1)
    n_blocks = (n * TOP_K) // MOE_BM + n_exp
    starts = jnp.arange(n_blocks, dtype=jnp.int32) * MOE_BM
    block_e = jnp.clip(jnp.searchsorted(p_ends, starts, side='right'), 0, n_exp - 1).astype(jnp.int32)
    block_valid = (starts < p_ends[-1]).astype(jnp.int32)

    xs = _dispatch(h2, dest, jnp.zeros((n_blocks * MOE_BM, d), F32))
    ys = _moe(xs, block_e, block_valid, w_gu, b_gu, w_dn, b_dn)
    y = _combine(x1, gw, g2, final_g, ys, dest, seq)
    return y.reshape(bsz, seq, d)


def kernel(x_prompt, x_sample, c_prompt, c_sample, ln1_g, ln2_g, w_ada, b_ada, w_in, b_gates, conv_w,
           conv_b, head_norm_g, w_four, w_out, w_router, b_router, w_gate_up, b_gate_up, w_down, b_down,
           final_g):
    assert w_ada.shape[0] == 1, "single-layer trunk"
    d = x_prompt.shape[-1]
    mw = d // 2
    cg = (d - mw) // N_GROUPS
    w_in0 = w_in[0]
    w_main = jnp.concatenate([w_in0[:, :4 * mw], w_in0[:, 4 * mw + N_GATES:]], axis=1).astype(BF16)
    w_gates = jnp.pad(w_in0[:, 4 * mw:4 * mw + N_GATES], ((0, 0), (0, LANES - N_GATES))).astype(BF16)
    bg = jnp.pad(b_gates[0], (0, LANES - N_GATES)).reshape(1, LANES)
    kc = np.arange(cg)
    angc = 2.0 * np.pi * np.outer(kc, kc) / cg
    mc = jnp.asarray(np.concatenate([np.cos(angc), -np.sin(angc)], axis=0) / math.sqrt(cg), BF16)
    wr = w_router[0]
    wr_hi = wr.astype(BF16)
    wr_lo = (wr - wr_hi.astype(F32)).astype(BF16)
    wts = (ln1_g[0], ln2_g[0], w_main, w_gates, bg, conv_w[0], conv_b[0], head_norm_g[0], w_four[0],
           w_out[0].astype(BF16), mc, wr_hi, wr_lo, b_router[0],
           w_gate_up[0].astype(BF16), b_gate_up[0], w_down[0].astype(BF16), b_down[0])
    nbp = c_prompt.shape[0]
    ada = _ada(jnp.concatenate([c_prompt, c_sample], axis=0), w_ada[0], b_ada[0])
    y_prompt = _encoder_layer(x_prompt, ada[:nbp], wts, final_g)
    y_sample = _encoder_layer(x_sample, ada[nbp:], wts, final_g)
    return (y_prompt, y_sample)
```

```python
import functools
import math

import numpy as np
import jax
import jax.numpy as jnp
from jax import lax
from jax.experimental import pallas as pl
from jax.experimental.pallas import tpu as pltpu

F32 = jnp.float32
BF16 = jnp.bfloat16

N_HEADS = 4
N_GROUPS = 4
N_GATES = 4 * N_HEADS
CHUNK = 128
TOP_K = 4
SWIGLU_LIMIT = 7.0
SWIGLU_ALPHA = 1.702
EPS = 1e-6
LANES = 128
SUBLANES = 8
DFT_N2 = 128
MOE_BM = 512
ROW_TM = 256
VMEM_LIMIT = 56 * 1024 * 1024


def _params(*sem):
    return pltpu.CompilerParams(dimension_semantics=sem, vmem_limit_bytes=VMEM_LIMIT)


def _tile(n, pref):
    t = min(n, pref)
    while n % t:
        t //= 2
    return t


def _ada_kernel(c_ref, w_ref, b_ref, o_ref):
    c = c_ref[...]
    s = (c * jax.nn.sigmoid(c)).astype(BF16)
    o_ref[...] = jnp.dot(s, w_ref[...].astype(BF16), preferred_element_type=F32) + b_ref[...]


def _ada(c, w_ada, b_ada):
    bsz, d = c.shape
    bp = -(-bsz // SUBLANES) * SUBLANES
    cp = jnp.pad(c, ((0, bp - bsz), (0, 0)))
    n_out = w_ada.shape[1]
    tn = _tile(n_out, 1024)
    out = pl.pallas_call(
        _ada_kernel,
        grid=(n_out // tn,),
        in_specs=[pl.BlockSpec((bp, d), lambda j: (0, 0)),
                  pl.BlockSpec((d, tn), lambda j: (0, j)),
                  pl.BlockSpec((1, tn), lambda j: (0, j))],
        out_specs=pl.BlockSpec((bp, tn), lambda j: (0, j)),
        out_shape=jax.ShapeDtypeStruct((bp, n_out), F32),
        compiler_params=_params("arbitrary"),
        name="ada",
    )(cp, w_ada, b_ada.reshape(1, n_out))
    return out[:bsz]


def _rms(x, g):
    return (x * lax.rsqrt(jnp.mean(x * x, axis=-1, keepdims=True) + EPS)) * g


def _inproj_kernel(x_ref, g_ref, sc_ref, sh_ref, w_ref, wg_ref, bg_ref, o_ref, og_ref, h_scr):
    @pl.when(pl.program_id(1) == 0)
    def _():
        h = _rms(x_ref[...], g_ref[...]) * (1.0 + sc_ref[0]) + sh_ref[0]
        hb = h.astype(BF16)
        h_scr[...] = hb
        og_ref[...] = jnp.dot(hb, wg_ref[...], preferred_element_type=F32) + bg_ref[...]

    o_ref[...] = jnp.dot(h_scr[...], w_ref[...], preferred_element_type=F32)


def _inproj(x2, ln_g, sc, sh, w_main, w_gates, b_gates, seq):
    n, d = x2.shape
    p = w_main.shape[1]
    tm = _tile(seq, 1024)
    tn = _tile(p, 1024)
    tps = seq // tm
    return pl.pallas_call(
        _inproj_kernel,
        grid=(n // tm, p // tn),
        in_specs=[pl.BlockSpec((tm, d), lambda i, j: (i, 0)),
                  pl.BlockSpec((1, d), lambda i, j: (0, 0)),
                  pl.BlockSpec((1, 1, d), lambda i, j: (i // tps, 0, 0)),
                  pl.BlockSpec((1, 1, d), lambda i, j: (i // tps, 0, 0)),
                  pl.BlockSpec((d, tn), lambda i, j: (0, j)),
                  pl.BlockSpec((d, LANES), lambda i, j: (0, 0)),
                  pl.BlockSpec((1, LANES), lambda i, j: (0, 0))],
        out_specs=[pl.BlockSpec((tm, tn), lambda i, j: (i, j)),
                   pl.BlockSpec((tm, LANES), lambda i, j: (i, 0))],
        out_shape=[jax.ShapeDtypeStruct((n, p), F32),
                   jax.ShapeDtypeStruct((n, LANES), F32)],
        scratch_shapes=[pltpu.VMEM((tm, d), BF16)],
        compiler_params=_params("arbitrary", "arbitrary"),
        name="inproj",
    )(x2, ln_g.reshape(1, d), sc, sh, w_main, w_gates, b_gates)


def _conv_kernel(x_ref, prev_ref, next_ref, w_ref, b_ref, o_ref, *, tps, k_scale):
    it = pl.program_id(0) % tps
    w = w_ref[...]
    b = b_ref[...]
    taps = w.shape[0]
    half = taps // 2
    scale = jnp.where(pl.program_id(1) == 1, k_scale, 1.0).astype(F32)

    def conv_act(z):
        nz = z.shape[0]
        acc = z * w[half:half + 1]
        for j in range(taps):
            if j != half:
                acc = acc + pltpu.roll(z, (half - j) % nz, 0) * w[j:j + 1]
        acc = acc + b
        return (acc * jax.nn.sigmoid(acc)) * scale

    x = x_ref[...]
    tm = x.shape[0]
    o_ref[...] = conv_act(x)
    prev = jnp.where(it == 0, 0.0, prev_ref[...])
    nxt = jnp.where(it == tps - 1, 0.0, next_ref[...])
    top = conv_act(jnp.concatenate([prev, x[0:2 * SUBLANES]], axis=0))
    o_ref[0:SUBLANES, :] = top[SUBLANES:2 * SUBLANES]
    bot = conv_act(jnp.concatenate([x[tm - 2 * SUBLANES:tm], nxt], axis=0))
    o_ref[tm - SUBLANES:tm, :] = bot[SUBLANES:2 * SUBLANES]


def _conv(proj, conv_w, conv_b, seq, mw, k_scale):
    n = proj.shape[0]
    taps = conv_w.shape[0]
    assert taps // 2 <= SUBLANES
    tm = _tile(seq, 512)
    assert tm >= 4 * SUBLANES
    tps = seq // tm
    r8 = tm // SUBLANES
    last8 = n // SUBLANES - 1
    return pl.pallas_call(
        functools.partial(_conv_kernel, tps=tps, k_scale=k_scale),
        grid=(n // tm, 2),
        in_specs=[pl.BlockSpec((tm, mw), lambda i, j: (i, j)),
                  pl.BlockSpec((SUBLANES, mw), lambda i, j: (jnp.maximum(i * r8 - 1, 0), j)),
                  pl.BlockSpec((SUBLANES, mw), lambda i, j: (jnp.minimum((i + 1) * r8, last8), j)),
                  pl.BlockSpec((taps, mw), lambda i, j: (0, j)),
                  pl.BlockSpec((1, mw), lambda i, j: (0, j))],
        out_specs=pl.BlockSpec((tm, mw), lambda i, j: (i, j)),
        out_shape=jax.ShapeDtypeStruct((n, 2 * mw), F32),
        compiler_params=_params("arbitrary", "arbitrary"),
        name="conv",
    )(proj, proj, proj, conv_w, conv_b.reshape(1, 2 * mw))


def _log_sigmoid(x):
    return jnp.minimum(x, 0.0) - jnp.log1p(jnp.exp(-jnp.abs(x)))


def _split3(x):
    hi = x.astype(BF16)
    r = x - hi.astype(F32)
    mid = r.astype(BF16)
    lo = (r - mid.astype(F32)).astype(BF16)
    return hi, mid, lo


def _mlstm_chunk(q, k, v, li_c, b_c, li_r, b_r, g, c_ref, n_ref, m_ref, idx, reverse):
    length = q.shape[0]
    m = m_ref[idx][:, 0:1]
    c_state = c_ref[idx]
    n_state = n_ref[idx]
    qb = q.astype(BF16)
    kb = k.astype(BF16)
    vb = v.astype(BF16)
    sqk = lax.dot_general(qb, kb, (((1,), (1,)), ((), ())), preferred_element_type=F32)
    row = lax.broadcasted_iota(jnp.int32, (length, length), 0)
    col = lax.broadcasted_iota(jnp.int32, (length, length), 1)
    mask = (col >= row) if reverse else (col <= row)
    d = jnp.where(mask, b_c - b_r + li_r, -jnp.inf)
    a = b_c + m
    m_t = jnp.maximum(a, jnp.max(d, axis=1, keepdims=True))
    wm = jnp.exp(d - m_t) * sqk
    e_a = jnp.exp(a - m_t)
    num = (jnp.dot(wm.astype(BF16), vb, preferred_element_type=F32)
           + e_a * jnp.dot(qb, c_state.astype(BF16), preferred_element_type=F32))
    den = (jnp.sum(wm, axis=1, keepdims=True)
           + e_a * jnp.sum(q * n_state, axis=1, keepdims=True))
    h = num / jnp.maximum(jnp.abs(den), jnp.exp(-m_t))
    w_end = g - b_c + li_c
    m_new = jnp.maximum(m + g, jnp.max(w_end, axis=0, keepdims=True))
    decay = jnp.exp(m + g - m_new)
    kw = k * jnp.exp(w_end - m_new)
    c_ref[idx] = decay * c_state + lax.dot_general(
        kw.astype(BF16), vb, (((0,), (0,)), ((), ())), preferred_element_type=F32)
    n_ref[idx] = decay * n_state + jnp.sum(kw, axis=0, keepdims=True)
    m_ref[idx] = jnp.broadcast_to(m_new, (1, LANES))
    return h


def _mlstm_kernel(qf_ref, kf_ref, vf_ref, gcf_ref, grf_ref,
                  qb_ref, kb_ref, vb_ref, gcb_ref, grb_ref,
                  l3_ref, u3_ref, l3t_ref, u3t_ref,
                  hf_ref, hb_ref, c_scr, n_scr, m_scr):
    @pl.when(pl.program_id(1) == 0)
    def _():
        c_scr[...] = jnp.zeros_like(c_scr)
        n_scr[...] = jnp.zeros_like(n_scr)
        m_scr[...] = jnp.zeros_like(m_scr)

    dh = qf_ref.shape[1] // N_HEADS
    length = qf_ref.shape[0]
    dirs = (
        (False, qf_ref, kf_ref, vf_ref, gcf_ref, grf_ref, l3_ref, u3t_ref, hf_ref, 0),
        (True, qb_ref, kb_ref, vb_ref, gcb_ref, grb_ref, u3_ref, l3t_ref, hb_ref, 2 * N_HEADS),
    )
    for reverse, q_ref, k_ref, v_ref, gc_ref, gr_ref, tri_c, tri_r, h_ref, col0 in dirs:
        gc = gc_ref[...]
        gr = gr_ref[...]
        b_cols = jnp.dot(tri_c[...], jnp.concatenate(_split3(_log_sigmoid(gc)), axis=0),
                         preferred_element_type=F32)
        b_rows = jnp.dot(jnp.concatenate(_split3(_log_sigmoid(gr)), axis=1), tri_r[...],
                         preferred_element_type=F32)
        for hd in range(N_HEADS):
            ci = col0 + hd
            cf = col0 + N_HEADS + hd
            b_c = b_cols[:, cf:cf + 1]
            g = b_c[0:1, :] if reverse else b_c[length - 1:length, :]
            sl = slice(hd * dh, (hd + 1) * dh)
            h = _mlstm_chunk(q_ref[:, sl], k_ref[:, sl], v_ref[:, sl],
                             gc[:, ci:ci + 1], b_c, gr[ci:ci + 1, :], b_rows[cf:cf + 1, :], g,
                             c_scr, n_scr, m_scr, (1 if reverse else 0) * N_HEADS + hd, reverse)
            h_ref[:, sl] = h


def _tri_consts(length):
    lower = np.tril(np.ones((length, length), np.float32))
    upper = lower.T
    l3 = np.concatenate([lower] * 3, axis=1)
    u3 = np.concatenate([upper] * 3, axis=1)
    l3t = np.concatenate([lower] * 3, axis=0)
    u3t = np.concatenate([upper] * 3, axis=0)
    return tuple(jnp.asarray(a, BF16) for a in (l3, u3, l3t, u3t))


def _mlstm(qk, proj, gates, gates_t, bsz, seq, mw):
    n = qk.shape[0]
    nc = seq // CHUNK
    dh = mw // N_HEADS
    fwd = lambda b, c: b * nc + c
    bwd = lambda b, c: b * nc + (nc - 1 - c)
    cst = lambda b, c: (0, 0)
    in_specs = []
    for pos in (fwd, bwd):
        in_specs += [
            pl.BlockSpec((CHUNK, mw), lambda b, c, pos=pos: (pos(b, c), 0)),
            pl.BlockSpec((CHUNK, mw), lambda b, c, pos=pos: (pos(b, c), 1)),
            pl.BlockSpec((CHUNK, mw), lambda b, c, pos=pos: (pos(b, c), 2)),
            pl.BlockSpec((CHUNK, LANES), lambda b, c, pos=pos: (pos(b, c), 0)),
            pl.BlockSpec((N_GATES, CHUNK), lambda b, c, pos=pos: (b, pos(0, c))),
        ]
    in_specs += [pl.BlockSpec((CHUNK, 3 * CHUNK), cst), pl.BlockSpec((CHUNK, 3 * CHUNK), cst),
                 pl.BlockSpec((3 * CHUNK, CHUNK), cst), pl.BlockSpec((3 * CHUNK, CHUNK), cst)]
    return pl.pallas_call(
        _mlstm_kernel,
        grid=(bsz, nc),
        in_specs=in_specs,
        out_specs=[pl.BlockSpec((CHUNK, mw), lambda b, c: (fwd(b, c), 0)),
                   pl.BlockSpec((CHUNK, mw), lambda b, c: (bwd(b, c), 0))],
        out_shape=[jax.ShapeDtypeStruct((n, mw), F32), jax.ShapeDtypeStruct((n, mw), F32)],
        scratch_shapes=[pltpu.VMEM((2 * N_HEADS, dh, dh), F32),
                        pltpu.VMEM((2 * N_HEADS, 1, dh), F32),
                        pltpu.VMEM((2 * N_HEADS, 1, LANES), F32)],
        compiler_params=_params("arbitrary", "arbitrary"),
        name="mlstm",
    )(qk, qk, proj, gates, gates_t, qk, qk, proj, gates, gates_t, *_tri_consts(CHUNK))


def _dft_a_kernel(x_ref, ga_ref, tc_ref, ts_ref, o_ref, *, jb):
    n1, width = x_ref.shape[1], x_ref.shape[3]
    rows = n1 * SUBLANES
    reps = width // LANES
    ga = ga_ref[...]
    for s in range(jb):
        sl = slice(s * SUBLANES, (s + 1) * SUBLANES)
        x = x_ref[0, :, sl, :].reshape(rows, width).astype(BF16)
        z = jnp.dot(ga, x, preferred_element_type=F32)
        zc, zs = z[:rows], z[rows:]
        tc = jnp.concatenate([tc_ref[s]] * reps, axis=1)
        ts = jnp.concatenate([ts_ref[s]] * reps, axis=1)
        o_ref[0, 0, :, sl, :] = (zc * tc - zs * ts).reshape(n1, SUBLANES, width)
        o_ref[0, 1, :, sl, :] = (zc * ts + zs * tc).reshape(n1, SUBLANES, width)


def _dft_b_kernel(z_ref, gb_ref, o_ref, scr):
    n2, wb = z_ref.shape[3], z_ref.shape[4]
    gb = gb_ref[...]
    for k in range(SUBLANES):
        slab = z_ref[0, :, k].reshape(2 * n2, wb).astype(BF16)
        p = jnp.dot(gb, slab, preferred_element_type=F32)
        for cs in range(2):
            for c in range(wb // LANES):
                scr[c, pl.ds(cs * n2 * SUBLANES + k, n2, stride=SUBLANES), :] = (
                    p[cs * n2:(cs + 1) * n2, c * LANES:(c + 1) * LANES])
    for c in range(wb // LANES):
        o_ref[0, :, :, :, c * LANES:(c + 1) * LANES] = scr[c].reshape(2, n2, SUBLANES, LANES)


def _position_dft(proj, bsz, seq, width, col_block):
    n2 = DFT_N2
    n1 = seq // n2
    assert seq % n2 == 0 and n1 % SUBLANES == 0 and proj.shape[1] % width == 0
    rows = n1 * SUBLANES
    k1 = np.arange(n1)
    ang1 = 2.0 * np.pi * np.outer(k1, k1) / n1
    eye = np.eye(SUBLANES)
    ga = jnp.asarray(np.concatenate([np.kron(np.cos(ang1), eye), np.kron(np.sin(ang1), eye)], axis=0)
                     / math.sqrt(n1), BF16)
    k2 = np.arange(n2)
    ang2 = 2.0 * np.pi * np.outer(k2, k2) / n2
    c2, s2 = np.cos(ang2), np.sin(ang2)
    gb = jnp.asarray(np.block([[c2, -s2], [s2, c2]]) / math.sqrt(n2), BF16)
    n2_idx = jnp.arange(n2, dtype=jnp.int32).reshape(n2 // SUBLANES, 1, SUBLANES)
    k1_idx = jnp.arange(n1, dtype=jnp.int32).reshape(1, n1, 1)
    ang = ((2.0 * math.pi / seq) * ((n2_idx * k1_idx) % seq).astype(F32)).reshape(n2 // SUBLANES, rows)
    tc = jnp.broadcast_to(jnp.cos(ang)[:, :, None], (n2 // SUBLANES, rows, LANES))
    ts = jnp.broadcast_to(jnp.sin(ang)[:, :, None], (n2 // SUBLANES, rows, LANES))

    jb = max(1, min(n2 // SUBLANES, 512 // rows))
    proj4 = proj.reshape(bsz, n1, n2, proj.shape[1])
    z = pl.pallas_call(
        functools.partial(_dft_a_kernel, jb=jb),
        grid=(bsz, n2 // (SUBLANES * jb)),
        in_specs=[pl.BlockSpec((1, n1, SUBLANES * jb, width), lambda b, j: (b, 0, j, col_block)),
                  pl.BlockSpec((2 * rows, rows), lambda b, j: (0, 0)),
                  pl.BlockSpec((jb, rows, LANES), lambda b, j: (j, 0, 0)),
                  pl.BlockSpec((jb, rows, LANES), lambda b, j: (j, 0, 0))],
        out_specs=pl.BlockSpec((1, 2, n1, SUBLANES * jb, width), lambda b, j: (b, 0, 0, j, 0)),
        out_shape=jax.ShapeDtypeStruct((bsz, 2, n1, n2, width), F32),
        compiler_params=_params("arbitrary", "arbitrary"),
        name="dft_a",
    )(proj4, ga, tc, ts)
    wb = _tile(width, 512)
    p = pl.pallas_call(
        _dft_b_kernel,
        grid=(bsz, n1 // SUBLANES, width // wb),
        in_specs=[pl.BlockSpec((1, 2, SUBLANES, n2, wb), lambda b, j, w: (b, 0, j, 0, w)),
                  pl.BlockSpec((2 * n2, 2 * n2), lambda b, j, w: (0, 0))],
        out_specs=pl.BlockSpec((1, 2, n2, SUBLANES, wb), lambda b, j, w: (b, 0, 0, j, w)),
        out_shape=jax.ShapeDtypeStruct((bsz, 2, n2, n1, width), F32),
        scratch_shapes=[pltpu.VMEM((wb // LANES, 2 * n2 * SUBLANES, LANES), F32)],
        compiler_params=_params("arbitrary", "arbitrary", "arbitrary"),
        name="dft_b",
    )(z, gb)
    return p.reshape(bsz, 2, seq, width)


def _pack_bf16_pairs(xb):
    c = xb.shape[1] // 2
    lo = pltpu.bitcast(xb[:, :c].astype(F32), jnp.uint32)
    hi = pltpu.bitcast(xb[:, c:].astype(F32), jnp.uint32)
    return (lo >> 16) | hi


def _unpack_bf16_pairs(w):
    lo = pltpu.bitcast(w << 16, F32).astype(BF16)
    hi = pltpu.bitcast(w & jnp.uint32(0xFFFF0000), F32).astype(BF16)
    return jnp.concatenate([lo, hi], axis=1)


def _mix_kernel(hf_ref, hb_ref, op_ref, pc_ref, ps_ref, x_ref, g1_ref, sc_ref, sh_ref,
                hng_ref, ln_ref, wout_ref, mc_ref, wf_ref, wrh_ref, wrl_ref, br_ref, ltri_ref, base0_ref,
                x1_ref, h2_ref, idx_ref, gw_ref, rank_ref, cnt_ref, base_scr):
    @pl.when(pl.program_id(0) == 0)
    def _():
        base_scr[...] = base0_ref[...]

    tm, mw = hf_ref.shape
    dh = mw // N_HEADS
    hm = hf_ref[...] + hb_ref[...]
    parts = []
    for hd in range(N_HEADS):
        seg = hm[:, hd * dh:(hd + 1) * dh]
        parts.append(seg * lax.rsqrt(jnp.mean(seg * seg, axis=-1, keepdims=True) + EPS))
    hm = jnp.concatenate(parts, axis=1) * hng_ref[...] * jax.nn.sigmoid(op_ref[...])
    cg = pc_ref.shape[3] // N_GROUPS
    pc = pc_ref[0, 0]
    ps = ps_ref[0, 0]
    for gi in range(N_GROUPS):
        sl = slice(gi * cg, (gi + 1) * cg)
        spec = jnp.dot(jnp.concatenate([pc[:, sl], ps[:, sl]], axis=1).astype(BF16), mc_ref[...],
                       preferred_element_type=F32)
        parts.append(jnp.dot(spec.astype(BF16), wf_ref[gi].astype(BF16), preferred_element_type=F32))
    cat = jnp.concatenate([hm] + parts[N_HEADS:], axis=1).astype(BF16)
    mix = jnp.dot(cat, wout_ref[...], preferred_element_type=F32)
    x1 = x_ref[...] + g1_ref[0] * mix
    x1_ref[...] = x1
    h2 = _rms(x1, ln_ref[...]) * (1.0 + sc_ref[0]) + sh_ref[0]
    hh = h2.astype(BF16)
    h2_ref[...] = _pack_bf16_pairs(hh)

    hl = (h2 - hh.astype(F32)).astype(BF16)
    wrh = wrh_ref[...]
    logits = (jnp.dot(hh, wrh, preferred_element_type=F32)
              + jnp.dot(hl, wrh, preferred_element_type=F32)
              + jnp.dot(hh, wrl_ref[...], preferred_element_type=F32)) + br_ref[...]
    n_exp = logits.shape[1]
    lane = lax.broadcasted_iota(jnp.int32, (tm, n_exp), 1).astype(F32)
    lane_o = lax.broadcasted_iota(jnp.int32, (tm, LANES), 1)
    base = base_scr[...]
    ltri = ltri_ref[...]
    idx_out = jnp.zeros((tm, LANES), F32)
    val_out = jnp.zeros((tm, LANES), F32)
    rank_out = jnp.zeros((tm, LANES), F32)
    top0 = None
    denom = jnp.zeros((tm, 1), F32)
    work = logits
    for k in range(TOP_K):
        mx = jnp.max(work, axis=1, keepdims=True)
        ik = jnp.min(jnp.where(work == mx, lane, float(n_exp)), axis=1, keepdims=True)
        hit = lane == ik
        work = jnp.where(hit, -jnp.inf, work)
        if top0 is None:
            top0 = mx
        ek = jnp.exp(mx - top0)
        denom = denom + ek
        onehot = hit.astype(F32)
        before = jnp.dot(ltri, onehot.astype(BF16), preferred_element_type=F32)
        rk = jnp.sum(jnp.where(hit, base + before, 0.0), axis=1, keepdims=True)
        base = base + jnp.sum(onehot, axis=0, keepdims=True)
        idx_out = jnp.where(lane_o == k, ik, idx_out)
        val_out = jnp.where(lane_o == k, ek, val_out)
        rank_out = jnp.where(lane_o == k, rk, rank_out)
    idx_ref[...] = idx_out.astype(jnp.int32)
    gw_ref[...] = val_out / denom
    rank_ref[...] = rank_out
    base_scr[...] = base
    cnt_ref[...] = base


def _mix(hf, hb, proj, pdft, x2, g1, sc2, sh2, hng, ln2_g, w_out, mc, w_four, wr_hi, wr_lo, b_router,
         base0, seq):
    n, d = x2.shape
    mw = hf.shape[1]
    fw = pdft.shape[3]
    n_exp = wr_hi.shape[1]
    cg = fw // N_GROUPS
    tm = _tile(seq, 256)
    tps = seq // tm
    ltri = jnp.asarray(np.tril(np.ones((tm, tm), np.float32), -1), BF16)
    row = lambda i: (i, 0)
    cst = lambda i: (0, 0)
    bat = lambda i: (i // tps, 0, 0)
    return pl.pallas_call(
        _mix_kernel,
        grid=(n // tm,),
        in_specs=[pl.BlockSpec((tm, mw), row), pl.BlockSpec((tm, mw), row),
                  pl.BlockSpec((tm, mw), lambda i: (i, 3)),
                  pl.BlockSpec((1, 1, tm, fw), lambda i: (i // tps, 0, i % tps, 0)),
                  pl.BlockSpec((1, 1, tm, fw), lambda i: (i // tps, 1, i % tps, 0)),
                  pl.BlockSpec((tm, d), row),
                  pl.BlockSpec((1, 1, d), bat), pl.BlockSpec((1, 1, d), bat), pl.BlockSpec((1, 1, d), bat),
                  pl.BlockSpec((1, mw), cst), pl.BlockSpec((1, d), cst),
                  pl.BlockSpec((mw + fw, d), cst),
                  pl.BlockSpec((2 * cg, cg), cst),
                  pl.BlockSpec((N_GROUPS, cg, cg), lambda i: (0, 0, 0)),
                  pl.BlockSpec((d, n_exp), cst), pl.BlockSpec((d, n_exp), cst),
                  pl.BlockSpec((1, n_exp), cst),
                  pl.BlockSpec((tm, tm), cst),
                  pl.BlockSpec((1, n_exp), cst)],
        out_specs=[pl.BlockSpec((tm, d), row), pl.BlockSpec((tm, d // 2), row),
                   pl.BlockSpec((tm, LANES), row), pl.BlockSpec((tm, LANES), row),
                   pl.BlockSpec((tm, LANES), row), pl.BlockSpec((1, n_exp), cst)],
        out_shape=[jax.ShapeDtypeStruct((n, d), F32), jax.ShapeDtypeStruct((n, d // 2), jnp.uint32),
                   jax.ShapeDtypeStruct((n, LANES), jnp.int32), jax.ShapeDtypeStruct((n, LANES), F32),
                   jax.ShapeDtypeStruct((n, LANES), F32), jax.ShapeDtypeStruct((1, n_exp), F32)],
        scratch_shapes=[pltpu.VMEM((1, n_exp), F32)],
        compiler_params=_params("arbitrary"),
        name="mix",
    )(hf, hb, proj, pdft, pdft, x2, g1, sc2, sh2, hng.reshape(1, mw), ln2_g.reshape(1, d),
      w_out, mc, w_four, wr_hi, wr_lo, b_router.reshape(1, n_exp), ltri, base0)


def _row_copy(src, src_row, dst, dst_row, sem):
    return pltpu.make_async_copy(src.at[pl.ds(src_row, 1), :], dst.at[pl.ds(dst_row, 1), :], sem)


def _dispatch_kernel(dest_ref, *refs, tiles):
    h_refs, (xs_ref, hbuf, sems) = refs[:len(tiles)], refs[len(tiles):]
    tm = hbuf.shape[1]
    i = pl.program_id(0)
    n_steps = pl.num_programs(0)
    slot = i % 2

    def drain(s):
        def body(t, carry):
            for _ in range(TOP_K):
                _row_copy(hbuf.at[s], 0, xs_ref, 0, sems.at[s]).wait()
            return carry
        lax.fori_loop(0, tm, body, 0)

    @pl.when(i >= 2)
    def _():
        drain(slot)

    first = 0
    for h_ref, cnt in zip(h_refs, tiles):
        @pl.when(jnp.logical_and(i >= first, i < first + cnt))
        def _(h_ref=h_ref):
            hbuf[slot] = h_ref[...]
        first += cnt

    @pl.when(i >= first)
    def _():
        hbuf[slot] = jnp.zeros(hbuf.shape[1:], hbuf.dtype)

    def issue(t, carry):
        for k in range(TOP_K):
            _row_copy(hbuf.at[slot], t, xs_ref, dest_ref[t * TOP_K + k], sems.at[slot]).start()
        return carry

    lax.fori_loop(0, tm, issue, 0)

    @pl.when(i == n_steps - 1)
    def _():
        drain(slot)

    @pl.when(jnp.logical_and(i == n_steps - 1, i >= 1))
    def _():
        drain(1 - slot)


def _dispatch(h2ps, dest_all, rows):
    w = h2ps[0].shape[1]
    tm = ROW_TM
    tiles = tuple(h.shape[0] // tm for h in h2ps)
    n_steps = dest_all.shape[0] // (tm * TOP_K)
    assert all(h.shape[0] % tm == 0 for h in h2ps) and dest_all.shape[0] == rows
    in_specs = [pl.BlockSpec((tm * TOP_K,), lambda i: (i,), memory_space=pltpu.SMEM)]
    first = 0
    for cnt in tiles:
        in_specs.append(pl.BlockSpec((tm, w), lambda i, first=first, cnt=cnt: (jnp.clip(i - first, 0, cnt - 1), 0)))
        first += cnt
    return pl.pallas_call(
        functools.partial(_dispatch_kernel, tiles=tiles),
        grid=(n_steps,),
        in_specs=in_specs,
        out_specs=pl.BlockSpec(memory_space=pl.ANY),
        out_shape=jax.ShapeDtypeStruct((rows, w), jnp.uint32),
        scratch_shapes=[pltpu.VMEM((2, tm, w), jnp.uint32), pltpu.SemaphoreType.DMA((2,))],
        compiler_params=_params("arbitrary"),
        name="dispatch",
    )(dest_all, *h2ps)


def _combine_kernel(dcur_ref, dnext_ref, x1_ref, gw_ref, g2_ref, fg_ref, ys_ref, o_ref, buf, sems):
    tm = x1_ref.shape[0]
    i = pl.program_id(0)
    n_steps = pl.num_programs(0)
    slot = i % 2

    def issue(d_ref, s):
        def body(t, carry):
            for k in range(TOP_K):
                _row_copy(ys_ref, d_ref[t * TOP_K + k], buf.at[s, k], t, sems.at[s]).start()
            return carry
        lax.fori_loop(0, tm, body, 0)

    @pl.when(i == 0)
    def _():
        issue(dcur_ref, slot)

    @pl.when(i + 1 < n_steps)
    def _():
        issue(dnext_ref, 1 - slot)

    def drain(t, carry):
        for k in range(TOP_K):
            _row_copy(ys_ref, 0, buf.at[slot, k], 0, sems.at[slot]).wait()
        return carry

    lax.fori_loop(0, tm, drain, 0)
    gw = gw_ref[...]
    y = gw[:, 0:1] * buf[slot, 0]
    for k in range(1, TOP_K):
        y = y + gw[:, k:k + 1] * buf[slot, k]
    x2 = x1_ref[...] + g2_ref[0] * y
    o_ref[...] = _rms(x2, fg_ref[...])


def _combine(x1, gw, g2, final_g, ys, dest, seq):
    n, d = x1.shape
    tm = _tile(seq, ROW_TM)
    tps = seq // tm
    last = n // tm - 1
    return pl.pallas_call(
        _combine_kernel,
        grid=(n // tm,),
        in_specs=[pl.BlockSpec((tm * TOP_K,), lambda i: (i,), memory_space=pltpu.SMEM),
                  pl.BlockSpec((tm * TOP_K,), lambda i: (jnp.minimum(i + 1, last),), memory_space=pltpu.SMEM),
                  pl.BlockSpec((tm, d), lambda i: (i, 0)),
                  pl.BlockSpec((tm, LANES), lambda i: (i, 0)),
                  pl.BlockSpec((1, 1, d), lambda i: (i // tps, 0, 0)),
                  pl.BlockSpec((1, d), lambda i: (0, 0)),
                  pl.BlockSpec(memory_space=pl.ANY)],
        out_specs=pl.BlockSpec((tm, d), lambda i: (i, 0)),
        out_shape=jax.ShapeDtypeStruct((n, d), F32),
        scratch_shapes=[pltpu.VMEM((2, TOP_K, tm, d), F32), pltpu.SemaphoreType.DMA((2,))],
        compiler_params=_params("arbitrary"),
        name="combine",
    )(dest, dest, x1, gw, g2, final_g.reshape(1, d), ys)


def _moe_kernel(be_ref, bv_ref, x_ref, wg_ref, wu_ref, wd_ref, bg_ref, bu_ref, bd_ref, o_ref, xb_scr):
    del be_ref
    valid = bv_ref[pl.program_id(0)] > 0

    @pl.when(pl.program_id(1) == 0)
    def _():
        o_ref[...] = jnp.broadcast_to(bd_ref[0], o_ref.shape)

    @pl.when(jnp.logical_and(valid, pl.program_id(1) == 0))
    def _():
        xb_scr[...] = _unpack_bf16_pairs(x_ref[...])

    @pl.when(valid)
    def _():
        xb = xb_scr[...]
        gate = jnp.dot(xb, wg_ref[0], preferred_element_type=F32) + bg_ref[0]
        up = jnp.dot(xb, wu_ref[0], preferred_element_type=F32) + bu_ref[0]
        gate = jnp.minimum(gate, SWIGLU_LIMIT)
        up = jnp.clip(up, -SWIGLU_LIMIT, SWIGLU_LIMIT)
        act = (up + 1.0) * (gate * jax.nn.sigmoid(SWIGLU_ALPHA * gate))
        o_ref[...] += jnp.dot(act.astype(BF16), wd_ref[0], preferred_element_type=F32)


def _moe(xs, block_e, block_valid, w_gu, b_gu, w_dn, b_dn):
    rows = xs.shape[0]
    n_exp, d_ff, d = w_dn.shape
    fc = _tile(d_ff, 512)
    nf = d_ff // fc
    n_blocks = rows // MOE_BM
    grid_spec = pltpu.PrefetchScalarGridSpec(
        num_scalar_prefetch=2,
        grid=(n_blocks, nf),
        in_specs=[pl.BlockSpec((MOE_BM, d // 2), lambda i, f, be, bv: (i, 0)),
                  pl.BlockSpec((1, d, fc), lambda i, f, be, bv: (be[i], 0, f)),
                  pl.BlockSpec((1, d, fc), lambda i, f, be, bv: (be[i], 0, nf + f)),
                  pl.BlockSpec((1, fc, d), lambda i, f, be, bv: (be[i], f, 0)),
                  pl.BlockSpec((1, 1, fc), lambda i, f, be, bv: (be[i], 0, f)),
                  pl.BlockSpec((1, 1, fc), lambda i, f, be, bv: (be[i], 0, nf + f)),
                  pl.BlockSpec((1, 1, d), lambda i, f, be, bv: (be[i], 0, 0))],
        out_specs=pl.BlockSpec((MOE_BM, d), lambda i, f, be, bv: (i, 0)),
        scratch_shapes=[pltpu.VMEM((MOE_BM, d), BF16)],
    )
    return pl.pallas_call(
        _moe_kernel,
        grid_spec=grid_spec,
        out_shape=jax.ShapeDtypeStruct((rows, d), F32),
        compiler_params=_params("arbitrary", "arbitrary"),
        name="moe",
    )(block_e, block_valid, xs, w_gu, w_gu, w_dn,
      b_gu.reshape(n_exp, 1, 2 * d_ff), b_gu.reshape(n_exp, 1, 2 * d_ff), b_dn.reshape(n_exp, 1, d))


def _mixer_and_router(x, ada, wts, base0):
    (ln1_g, ln2_g, w_main, w_gates, b_gates, conv_w, conv_b, head_norm_g, w_four, w_out_b, mc,
     wr_hi, wr_lo, b_router) = wts
    bsz, seq, d = x.shape
    n = bsz * seq
    mw = d // 2
    dh = mw // N_HEADS
    x2 = x.reshape(n, d)
    sh1, sc1, g1, sh2, sc2, g2 = [a.reshape(bsz, 1, d) for a in jnp.split(ada, 6, axis=-1)]

    proj, gates = _inproj(x2, ln1_g, sc1, sh1, w_main, w_gates, b_gates, seq)
    qk = _conv(proj, conv_w, conv_b, seq, mw, dh ** -0.5)
    gates_t = gates[:, :N_GATES].reshape(bsz, seq, N_GATES).transpose(0, 2, 1).reshape(bsz * N_GATES, seq)
    hf, hb = _mlstm(qk, proj, gates, gates_t, bsz, seq, mw)
    assert d - mw == mw, "the Fourier block is addressed as column block 4 of proj"
    pdft = _position_dft(proj, bsz, seq, mw, 4)
    x1, h2p, idx, gw, rank, cnt = _mix(hf, hb, proj, pdft, x2, g1, sc2, sh2, head_norm_g, ln2_g,
                                       w_out_b, mc, w_four, wr_hi, wr_lo, b_router, base0, seq)
    return dict(x1=x1, h2p=h2p, idx=idx[:, :TOP_K], gw=gw, rank=rank[:, :TOP_K].astype(jnp.int32),
                cnt=cnt, g2=g2, shape=(bsz, seq, d))


def kernel(x_prompt, x_sample, c_prompt, c_sample, ln1_g, ln2_g, w_ada, b_ada, w_in, b_gates, conv_w,
           conv_b, head_norm_g, w_four, w_out, w_router, b_router, w_gate_up, b_gate_up, w_down, b_down,
           final_g):
    assert w_ada.shape[0] == 1, "single-layer trunk"
    d = x_prompt.shape[-1]
    mw = d // 2
    cg = (d - mw) // N_GROUPS
    w_in0 = w_in[0]
    w_main = jnp.concatenate([w_in0[:, :4 * mw], w_in0[:, 4 * mw + N_GATES:]], axis=1).astype(BF16)
    w_gates = jnp.pad(w_in0[:, 4 * mw:4 * mw + N_GATES], ((0, 0), (0, LANES - N_GATES))).astype(BF16)
    bg = jnp.pad(b_gates[0], (0, LANES - N_GATES)).reshape(1, LANES)
    kc = np.arange(cg)
    angc = 2.0 * np.pi * np.outer(kc, kc) / cg
    mc = jnp.asarray(np.concatenate([np.cos(angc), -np.sin(angc)], axis=0) / math.sqrt(cg), BF16)
    wr = w_router[0]
    wr_hi = wr.astype(BF16)
    wr_lo = (wr - wr_hi.astype(F32)).astype(BF16)
    wts = (ln1_g[0], ln2_g[0], w_main, w_gates, bg, conv_w[0], conv_b[0], head_norm_g[0], w_four[0],
           w_out[0].astype(BF16), mc, wr_hi, wr_lo, b_router[0])
    n_exp = wr.shape[1]
    nbp = c_prompt.shape[0]
    ada = _ada(jnp.concatenate([c_prompt, c_sample], axis=0), w_ada[0], b_ada[0])
    gp = _mixer_and_router(x_prompt, ada[:nbp], wts, jnp.zeros((1, n_exp), F32))
    gs = _mixer_and_router(x_sample, ada[nbp:], wts, gp["cnt"])
    groups = (gp, gs)

    counts = gs["cnt"][0].astype(jnp.int32)
    p_counts = (counts + MOE_BM - 1) // MOE_BM * MOE_BM
    p_ends = jnp.cumsum(p_counts)
    p_off = p_ends - p_counts
    n_tok = sum(g["x1"].shape[0] for g in groups)
    n_blocks = (n_tok * TOP_K) // MOE_BM + n_exp
    starts = jnp.arange(n_blocks, dtype=jnp.int32) * MOE_BM
    block_e = jnp.minimum(jnp.sum((starts[:, None] >= p_ends[None, :]).astype(jnp.int32), axis=1), n_exp - 1)
    block_valid = (starts < p_ends[-1]).astype(jnp.int32)
    dests = [(p_off[g["idx"]] + g["rank"]).reshape(-1) for g in groups]
    rows = n_blocks * MOE_BM
    gap_start = jnp.concatenate([p_off + counts, p_ends[-1:]])
    gap_len = jnp.concatenate([p_counts - counts, rows - p_ends[-1:]])
    gap_end = jnp.cumsum(gap_len)
    j = jnp.arange(n_exp * MOE_BM, dtype=jnp.int32)
    gi = jnp.sum((j[:, None] >= gap_end[None, :]).astype(jnp.int32), axis=1)
    pad_dest = (gap_start[gi] + (j - (gap_end - gap_len)[gi])).astype(jnp.int32)

    xs = _dispatch([g["h2p"] for g in groups], jnp.concatenate(dests + [pad_dest]), rows)
    ys = _moe(xs, block_e, block_valid, w_gate_up[0].astype(BF16), b_gate_up[0], w_down[0].astype(BF16), b_down[0])
    outs = []
    for g, dest in zip(groups, dests):
        bsz, seq, _ = g["shape"]
        outs.append(_combine(g["x1"], g["gw"], g["g2"], final_g, ys, dest, seq).reshape(g["shape"]))
    return tuple(outs)
```

```python
import functools
import math

import numpy as np
import jax
import jax.numpy as jnp
from jax import lax
from jax.experimental import pallas as pl
from jax.experimental.pallas import tpu as pltpu

F32 = jnp.float32
BF16 = jnp.bfloat16

N_HEADS = 4
N_GROUPS = 4
N_GATES = 4 * N_HEADS
CHUNK = 128
TOP_K = 4
SWIGLU_LIMIT = 7.0
SWIGLU_ALPHA = 1.702
EPS = 1e-6
LANES = 128
SUBLANES = 8
DFT_N2 = 128
MOE_BM = 512
ROW_TM = 256
VMEM_LIMIT = 56 * 1024 * 1024


def _params(*sem):
    return pltpu.CompilerParams(dimension_semantics=sem, vmem_limit_bytes=VMEM_LIMIT)


def _tile(n, pref):
    t = min(n, pref)
    while n % t:
        t //= 2
    return t


def _ada_kernel(c_ref, w_ref, b_ref, o_ref):
    c = c_ref[...]
    s = (c * jax.nn.sigmoid(c)).astype(BF16)
    o_ref[...] = jnp.dot(s, w_ref[...].astype(BF16), preferred_element_type=F32) + b_ref[...]


def _ada(c, w_ada, b_ada):
    bsz, d = c.shape
    bp = -(-bsz // SUBLANES) * SUBLANES
    cp = jnp.pad(c, ((0, bp - bsz), (0, 0)))
    n_out = w_ada.shape[1]
    tn = _tile(n_out, 1024)
    out = pl.pallas_call(
        _ada_kernel,
        grid=(n_out // tn,),
        in_specs=[pl.BlockSpec((bp, d), lambda j: (0, 0)),
                  pl.BlockSpec((d, tn), lambda j: (0, j)),
                  pl.BlockSpec((1, tn), lambda j: (0, j))],
        out_specs=pl.BlockSpec((bp, tn), lambda j: (0, j)),
        out_shape=jax.ShapeDtypeStruct((bp, n_out), F32),
        compiler_params=_params("arbitrary"),
        name="ada",
    )(cp, w_ada, b_ada.reshape(1, n_out))
    return out[:bsz]


def _rms(x, g):
    return (x * lax.rsqrt(jnp.mean(x * x, axis=-1, keepdims=True) + EPS)) * g


def _inproj_kernel(x_ref, g_ref, sc_ref, sh_ref, w_ref, wg_ref, bg_ref, o_ref, og_ref, h_scr):
    @pl.when(pl.program_id(1) == 0)
    def _():
        h = _rms(x_ref[...], g_ref[...]) * (1.0 + sc_ref[0]) + sh_ref[0]
        hb = h.astype(BF16)
        h_scr[...] = hb
        og_ref[...] = jnp.dot(hb, wg_ref[...], preferred_element_type=F32) + bg_ref[...]

    o_ref[...] = jnp.dot(h_scr[...], w_ref[...], preferred_element_type=F32)


def _inproj(x2, ln_g, sc, sh, w_main, w_gates, b_gates, seq):
    n, d = x2.shape
    p = w_main.shape[1]
    tm = _tile(seq, 1024)
    tn = _tile(p, 1024)
    tps = seq // tm
    return pl.pallas_call(
        _inproj_kernel,
        grid=(n // tm, p // tn),
        in_specs=[pl.BlockSpec((tm, d), lambda i, j: (i, 0)),
                  pl.BlockSpec((1, d), lambda i, j: (0, 0)),
                  pl.BlockSpec((1, 1, d), lambda i, j: (i // tps, 0, 0)),
                  pl.BlockSpec((1, 1, d), lambda i, j: (i // tps, 0, 0)),
                  pl.BlockSpec((d, tn), lambda i, j: (0, j)),
                  pl.BlockSpec((d, LANES), lambda i, j: (0, 0)),
                  pl.BlockSpec((1, LANES), lambda i, j: (0, 0))],
        out_specs=[pl.BlockSpec((tm, tn), lambda i, j: (i, j)),
                   pl.BlockSpec((tm, LANES), lambda i, j: (i, 0))],
        out_shape=[jax.ShapeDtypeStruct((n, p), F32),
                   jax.ShapeDtypeStruct((n, LANES), F32)],
        scratch_shapes=[pltpu.VMEM((tm, d), BF16)],
        compiler_params=_params("arbitrary", "arbitrary"),
        name="inproj",
    )(x2, ln_g.reshape(1, d), sc, sh, w_main, w_gates, b_gates)


def _conv_kernel(x_ref, prev_ref, next_ref, w_ref, b_ref, o_ref, *, tps, k_scale):
    it = pl.program_id(0) % tps
    w = w_ref[...]
    b = b_ref[...]
    taps = w.shape[0]
    half = taps // 2
    scale = jnp.where(pl.program_id(1) == 1, k_scale, 1.0).astype(F32)

    def conv_act(z):
        nz = z.shape[0]
        acc = z * w[half:half + 1]
        for j in range(taps):
            if j != half:
                acc = acc + pltpu.roll(z, (half - j) % nz, 0) * w[j:j + 1]
        acc = acc + b
        return (acc * jax.nn.sigmoid(acc)) * scale

    x = x_ref[...]
    tm = x.shape[0]
    o_ref[...] = conv_act(x)
    prev = jnp.where(it == 0, 0.0, prev_ref[...])
    nxt = jnp.where(it == tps - 1, 0.0, next_ref[...])
    top = conv_act(jnp.concatenate([prev, x[0:2 * SUBLANES]], axis=0))
    o_ref[0:SUBLANES, :] = top[SUBLANES:2 * SUBLANES]
    bot = conv_act(jnp.concatenate([x[tm - 2 * SUBLANES:tm], nxt], axis=0))
    o_ref[tm - SUBLANES:tm, :] = bot[SUBLANES:2 * SUBLANES]


def _conv(proj, conv_w, conv_b, seq, mw, k_scale):
    n = proj.shape[0]
    taps = conv_w.shape[0]
    assert taps // 2 <= SUBLANES
    tm = _tile(seq, 512)
    assert tm >= 4 * SUBLANES
    tps = seq // tm
    r8 = tm // SUBLANES
    last8 = n // SUBLANES - 1
    return pl.pallas_call(
        functools.partial(_conv_kernel, tps=tps, k_scale=k_scale),
        grid=(n // tm, 2),
        in_specs=[pl.BlockSpec((tm, mw), lambda i, j: (i, j)),
                  pl.BlockSpec((SUBLANES, mw), lambda i, j: (jnp.maximum(i * r8 - 1, 0), j)),
                  pl.BlockSpec((SUBLANES, mw), lambda i, j: (jnp.minimum((i + 1) * r8, last8), j)),
                  pl.BlockSpec((taps, mw), lambda i, j: (0, j)),
                  pl.BlockSpec((1, mw), lambda i, j: (0, j))],
        out_specs=pl.BlockSpec((tm, mw), lambda i, j: (i, j)),
        out_shape=jax.ShapeDtypeStruct((n, 2 * mw), F32),
        compiler_params=_params("arbitrary", "arbitrary"),
        name="conv",
    )(proj, proj, proj, conv_w, conv_b.reshape(1, 2 * mw))


def _log_sigmoid(x):
    return jnp.minimum(x, 0.0) - jnp.log1p(jnp.exp(-jnp.abs(x)))


def _split3(x):
    hi = x.astype(BF16)
    r = x - hi.astype(F32)
    mid = r.astype(BF16)
    lo = (r - mid.astype(F32)).astype(BF16)
    return hi, mid, lo


def _mlstm_chunk(q, k, v, li_c, b_c, li_r, b_r, g, c_ref, n_ref, m_ref, idx, reverse):
    length = q.shape[0]
    m = m_ref[idx][:, 0:1]
    c_state = c_ref[idx]
    n_state = n_ref[idx]
    qb = q.astype(BF16)
    kb = k.astype(BF16)
    vb = v.astype(BF16)
    sqk = lax.dot_general(qb, kb, (((1,), (1,)), ((), ())), preferred_element_type=F32)
    row = lax.broadcasted_iota(jnp.int32, (length, length), 0)
    col = lax.broadcasted_iota(jnp.int32, (length, length), 1)
    mask = (col >= row) if reverse else (col <= row)
    d = jnp.where(mask, b_c - b_r + li_r, -jnp.inf)
    a = b_c + m
    m_t = jnp.maximum(a, jnp.max(d, axis=1, keepdims=True))
    wm = jnp.exp(d - m_t) * sqk
    e_a = jnp.exp(a - m_t)
    num = (jnp.dot(wm.astype(BF16), vb, preferred_element_type=F32)
           + e_a * jnp.dot(qb, c_state.astype(BF16), preferred_element_type=F32))
    den = (jnp.sum(wm, axis=1, keepdims=True)
           + e_a * jnp.sum(q * n_state, axis=1, keepdims=True))
    h = num / jnp.maximum(jnp.abs(den), jnp.exp(-m_t))
    w_end = g - b_c + li_c
    m_new = jnp.maximum(m + g, jnp.max(w_end, axis=0, keepdims=True))
    decay = jnp.exp(m + g - m_new)
    kw = k * jnp.exp(w_end - m_new)
    c_ref[idx] = decay * c_state + lax.dot_general(
        kw.astype(BF16), vb, (((0,), (0,)), ((), ())), preferred_element_type=F32)
    n_ref[idx] = decay * n_state + jnp.sum(kw, axis=0, keepdims=True)
    m_ref[idx] = jnp.broadcast_to(m_new, (1, LANES))
    return h


def _mlstm_kernel(qf_ref, kf_ref, vf_ref, gcf_ref, grf_ref,
                  qb_ref, kb_ref, vb_ref, gcb_ref, grb_ref,
                  l3_ref, u3_ref, l3t_ref, u3t_ref,
                  hf_ref, hb_ref, c_scr, n_scr, m_scr):
    @pl.when(pl.program_id(1) == 0)
    def _():
        c_scr[...] = jnp.zeros_like(c_scr)
        n_scr[...] = jnp.zeros_like(n_scr)
        m_scr[...] = jnp.zeros_like(m_scr)

    dh = qf_ref.shape[1] // N_HEADS
    length = qf_ref.shape[0]
    dirs = (
        (False, qf_ref, kf_ref, vf_ref, gcf_ref, grf_ref, l3_ref, u3t_ref, hf_ref, 0),
        (True, qb_ref, kb_ref, vb_ref, gcb_ref, grb_ref, u3_ref, l3t_ref, hb_ref, 2 * N_HEADS),
    )
    for reverse, q_ref, k_ref, v_ref, gc_ref, gr_ref, tri_c, tri_r, h_ref, col0 in dirs:
        gc = gc_ref[...]
        gr = gr_ref[...]
        b_cols = jnp.dot(tri_c[...], jnp.concatenate(_split3(_log_sigmoid(gc)), axis=0),
                         preferred_element_type=F32)
        b_rows = jnp.dot(jnp.concatenate(_split3(_log_sigmoid(gr)), axis=1), tri_r[...],
                         preferred_element_type=F32)
        for hd in range(N_HEADS):
            ci = col0 + hd
            cf = col0 + N_HEADS + hd
            b_c = b_cols[:, cf:cf + 1]
            g = b_c[0:1, :] if reverse else b_c[length - 1:length, :]
            sl = slice(hd * dh, (hd + 1) * dh)
            h = _mlstm_chunk(q_ref[:, sl], k_ref[:, sl], v_ref[:, sl],
                             gc[:, ci:ci + 1], b_c, gr[ci:ci + 1, :], b_rows[cf:cf + 1, :], g,
                             c_scr, n_scr, m_scr, (1 if reverse else 0) * N_HEADS + hd, reverse)
            h_ref[:, sl] = h


def _tri_consts(length):
    lower = np.tril(np.ones((length, length), np.float32))
    upper = lower.T
    l3 = np.concatenate([lower] * 3, axis=1)
    u3 = np.concatenate([upper] * 3, axis=1)
    l3t = np.concatenate([lower] * 3, axis=0)
    u3t = np.concatenate([upper] * 3, axis=0)
    return tuple(jnp.asarray(a, BF16) for a in (l3, u3, l3t, u3t))


def _mlstm(qk, proj, gates, gates_t, bsz, seq, mw):
    n = qk.shape[0]
    nc = seq // CHUNK
    dh = mw // N_HEADS
    fwd = lambda b, c: b * nc + c
    bwd = lambda b, c: b * nc + (nc - 1 - c)
    cst = lambda b, c: (0, 0)
    in_specs = []
    for pos in (fwd, bwd):
        in_specs += [
            pl.BlockSpec((CHUNK, mw), lambda b, c, pos=pos: (pos(b, c), 0)),
            pl.BlockSpec((CHUNK, mw), lambda b, c, pos=pos: (pos(b, c), 1)),
            pl.BlockSpec((CHUNK, mw), lambda b, c, pos=pos: (pos(b, c), 2)),
            pl.BlockSpec((CHUNK, LANES), lambda b, c, pos=pos: (pos(b, c), 0)),
            pl.BlockSpec((N_GATES, CHUNK), lambda b, c, pos=pos: (b, pos(0, c))),
        ]
    in_specs += [pl.BlockSpec((CHUNK, 3 * CHUNK), cst), pl.BlockSpec((CHUNK, 3 * CHUNK), cst),
                 pl.BlockSpec((3 * CHUNK, CHUNK), cst), pl.BlockSpec((3 * CHUNK, CHUNK), cst)]
    return pl.pallas_call(
        _mlstm_kernel,
        grid=(bsz, nc),
        in_specs=in_specs,
        out_specs=[pl.BlockSpec((CHUNK, mw), lambda b, c: (fwd(b, c), 0)),
                   pl.BlockSpec((CHUNK, mw), lambda b, c: (bwd(b, c), 0))],
        out_shape=[jax.ShapeDtypeStruct((n, mw), F32), jax.ShapeDtypeStruct((n, mw), F32)],
        scratch_shapes=[pltpu.VMEM((2 * N_HEADS, dh, dh), F32),
                        pltpu.VMEM((2 * N_HEADS, 1, dh), F32),
                        pltpu.VMEM((2 * N_HEADS, 1, LANES), F32)],
        compiler_params=_params("arbitrary", "arbitrary"),
        name="mlstm",
    )(qk, qk, proj, gates, gates_t, qk, qk, proj, gates, gates_t, *_tri_consts(CHUNK))


def _dft_a_kernel(x_ref, ga_ref, tc_ref, ts_ref, o_ref, *, jb):
    n1, width = x_ref.shape[1], x_ref.shape[3]
    rows = n1 * SUBLANES
    reps = width // LANES
    ga = ga_ref[...]
    for s in range(jb):
        sl = slice(s * SUBLANES, (s + 1) * SUBLANES)
        x = x_ref[0, :, sl, :].reshape(rows, width).astype(BF16)
        z = jnp.dot(ga, x, preferred_element_type=F32)
        zc, zs = z[:rows], z[rows:]
        tc = jnp.concatenate([tc_ref[s]] * reps, axis=1)
        ts = jnp.concatenate([ts_ref[s]] * reps, axis=1)
        o_ref[0, 0, :, sl, :] = (zc * tc - zs * ts).reshape(n1, SUBLANES, width)
        o_ref[0, 1, :, sl, :] = (zc * ts + zs * tc).reshape(n1, SUBLANES, width)


def _dft_b_kernel(z_ref, gb_ref, o_ref, scr):
    n2, wb = z_ref.shape[3], z_ref.shape[4]
    gb = gb_ref[...]
    for k in range(SUBLANES):
        slab = z_ref[0, :, k].reshape(2 * n2, wb).astype(BF16)
        p = jnp.dot(gb, slab, preferred_element_type=F32)
        for cs in range(2):
            for c in range(wb // LANES):
                scr[c, pl.ds(cs * n2 * SUBLANES + k, n2, stride=SUBLANES), :] = (
                    p[cs * n2:(cs + 1) * n2, c * LANES:(c + 1) * LANES])
    for c in range(wb // LANES):
        o_ref[0, :, :, :, c * LANES:(c + 1) * LANES] = scr[c].reshape(2, n2, SUBLANES, LANES)


def _position_dft(proj, bsz, seq, width, col_block):
    n2 = DFT_N2
    n1 = seq // n2
    assert seq % n2 == 0 and n1 % SUBLANES == 0 and proj.shape[1] % width == 0
    rows = n1 * SUBLANES
    k1 = np.arange(n1)
    ang1 = 2.0 * np.pi * np.outer(k1, k1) / n1
    eye = np.eye(SUBLANES)
    ga = jnp.asarray(np.concatenate([np.kron(np.cos(ang1), eye), np.kron(np.sin(ang1), eye)], axis=0)
                     / math.sqrt(n1), BF16)
    k2 = np.arange(n2)
    ang2 = 2.0 * np.pi * np.outer(k2, k2) / n2
    c2, s2 = np.cos(ang2), np.sin(ang2)
    gb = jnp.asarray(np.block([[c2, -s2], [s2, c2]]) / math.sqrt(n2), BF16)
    n2_idx = jnp.arange(n2, dtype=jnp.int32).reshape(n2 // SUBLANES, 1, SUBLANES)
    k1_idx = jnp.arange(n1, dtype=jnp.int32).reshape(1, n1, 1)
    ang = ((2.0 * math.pi / seq) * ((n2_idx * k1_idx) % seq).astype(F32)).reshape(n2 // SUBLANES, rows)
    tc = jnp.broadcast_to(jnp.cos(ang)[:, :, None], (n2 // SUBLANES, rows, LANES))
    ts = jnp.broadcast_to(jnp.sin(ang)[:, :, None], (n2 // SUBLANES, rows, LANES))

    jb = max(1, min(n2 // SUBLANES, 512 // rows))
    proj4 = proj.reshape(bsz, n1, n2, proj.shape[1])
    z = pl.pallas_call(
        functools.partial(_dft_a_kernel, jb=jb),
        grid=(bsz, n2 // (SUBLANES * jb)),
        in_specs=[pl.BlockSpec((1, n1, SUBLANES * jb, width), lambda b, j: (b, 0, j, col_block)),
                  pl.BlockSpec((2 * rows, rows), lambda b, j: (0, 0)),
                  pl.BlockSpec((jb, rows, LANES), lambda b, j: (j, 0, 0)),
                  pl.BlockSpec((jb, rows, LANES), lambda b, j: (j, 0, 0))],
        out_specs=pl.BlockSpec((1, 2, n1, SUBLANES * jb, width), lambda b, j: (b, 0, 0, j, 0)),
        out_shape=jax.ShapeDtypeStruct((bsz, 2, n1, n2, width), F32),
        compiler_params=_params("arbitrary", "arbitrary"),
        name="dft_a",
    )(proj4, ga, tc, ts)
    wb = _tile(width, 512)
    p = pl.pallas_call(
        _dft_b_kernel,
        grid=(bsz, n1 // SUBLANES, width // wb),
        in_specs=[pl.BlockSpec((1, 2, SUBLANES, n2, wb), lambda b, j, w: (b, 0, j, 0, w)),
                  pl.BlockSpec((2 * n2, 2 * n2), lambda b, j, w: (0, 0))],
        out_specs=pl.BlockSpec((1, 2, n2, SUBLANES, wb), lambda b, j, w: (b, 0, 0, j, w)),
        out_shape=jax.ShapeDtypeStruct((bsz, 2, n2, n1, width), F32),
        scratch_shapes=[pltpu.VMEM((wb // LANES, 2 * n2 * SUBLANES, LANES), F32)],
        compiler_params=_params("arbitrary", "arbitrary", "arbitrary"),
        name="dft_b",
    )(z, gb)
    return p.reshape(bsz, 2, seq, width)


def _pack_bf16_pairs(xb):
    c = xb.shape[1] // 2
    lo = pltpu.bitcast(xb[:, :c].astype(F32), jnp.uint32)
    hi = pltpu.bitcast(xb[:, c:].astype(F32), jnp.uint32)
    return (lo >> 16) | hi


def _unpack_bf16_pairs(w):
    lo = pltpu.bitcast(w << 16, F32).astype(BF16)
    hi = pltpu.bitcast(w & jnp.uint32(0xFFFF0000), F32).astype(BF16)
    return jnp.concatenate([lo, hi], axis=1)


def _mix_kernel(hf_ref, hb_ref, op_ref, pc_ref, ps_ref, x_ref, g1_ref, sc_ref, sh_ref,
                hng_ref, ln_ref, wout_ref, mc_ref, wf_ref, wrh_ref, wrl_ref, br_ref, ltri_ref, base0_ref,
                x1_ref, h2_ref, idx_ref, gw_ref, rank_ref, cnt_ref, base_scr):
    @pl.when(pl.program_id(0) == 0)
    def _():
        base_scr[...] = base0_ref[...]

    tm, mw = hf_ref.shape
    dh = mw // N_HEADS
    hm = hf_ref[...] + hb_ref[...]
    parts = []
    for hd in range(N_HEADS):
        seg = hm[:, hd * dh:(hd + 1) * dh]
        parts.append(seg * lax.rsqrt(jnp.mean(seg * seg, axis=-1, keepdims=True) + EPS))
    hm = jnp.concatenate(parts, axis=1) * hng_ref[...] * jax.nn.sigmoid(op_ref[...])
    cg = pc_ref.shape[3] // N_GROUPS
    pc = pc_ref[0, 0]
    ps = ps_ref[0, 0]
    for gi in range(N_GROUPS):
        sl = slice(gi * cg, (gi + 1) * cg)
        spec = jnp.dot(jnp.concatenate([pc[:, sl], ps[:, sl]], axis=1).astype(BF16), mc_ref[...],
                       preferred_element_type=F32)
        parts.append(jnp.dot(spec.astype(BF16), wf_ref[gi].astype(BF16), preferred_element_type=F32))
    cat = jnp.concatenate([hm] + parts[N_HEADS:], axis=1).astype(BF16)
    mix = jnp.dot(cat, wout_ref[...], preferred_element_type=F32)
    x1 = x_ref[...] + g1_ref[0] * mix
    x1_ref[...] = x1
    h2 = _rms(x1, ln_ref[...]) * (1.0 + sc_ref[0]) + sh_ref[0]
    hh = h2.astype(BF16)
    _store_tokens(h2_ref, _pack_bf16_pairs(hh))

    hl = (h2 - hh.astype(F32)).astype(BF16)
    wrh = wrh_ref[...]
    logits = (jnp.dot(hh, wrh, preferred_element_type=F32)
              + jnp.dot(hl, wrh, preferred_element_type=F32)
              + jnp.dot(hh, wrl_ref[...], preferred_element_type=F32)) + br_ref[...]
    n_exp = logits.shape[1]
    lane = lax.broadcasted_iota(jnp.int32, (tm, n_exp), 1).astype(F32)
    lane_o = lax.broadcasted_iota(jnp.int32, (tm, LANES), 1)
    base = base_scr[...]
    ltri = ltri_ref[...]
    idx_out = jnp.zeros((tm, LANES), F32)
    val_out = jnp.zeros((tm, LANES), F32)
    rank_out = jnp.zeros((tm, LANES), F32)
    top0 = None
    denom = jnp.zeros((tm, 1), F32)
    work = logits
    for k in range(TOP_K):
        mx = jnp.max(work, axis=1, keepdims=True)
        ik = jnp.min(jnp.where(work == mx, lane, float(n_exp)), axis=1, keepdims=True)
        hit = lane == ik
        work = jnp.where(hit, -jnp.inf, work)
        if top0 is None:
            top0 = mx
        ek = jnp.exp(mx - top0)
        denom = denom + ek
        onehot = hit.astype(F32)
        before = jnp.dot(ltri, onehot.astype(BF16), preferred_element_type=F32)
        rk = jnp.sum(jnp.where(hit, base + before, 0.0), axis=1, keepdims=True)
        base = base + jnp.sum(onehot, axis=0, keepdims=True)
        idx_out = jnp.where(lane_o == k, ik, idx_out)
        val_out = jnp.where(lane_o == k, ek, val_out)
        rank_out = jnp.where(lane_o == k, rk, rank_out)
    idx_ref[...] = idx_out.astype(jnp.int32)
    gw_ref[...] = val_out / denom
    rank_ref[...] = rank_out
    base_scr[...] = base
    cnt_ref[...] = base


def _mix(hf, hb, proj, pdft, x2, g1, sc2, sh2, hng, ln2_g, w_out, mc, w_four, wr_hi, wr_lo, b_router,
         base0, seq):
    n, d = x2.shape
    mw = hf.shape[1]
    fw = pdft.shape[3]
    n_exp = wr_hi.shape[1]
    cg = fw // N_GROUPS
    tm = _tile(seq, 256)
    tps = seq // tm
    ltri = jnp.asarray(np.tril(np.ones((tm, tm), np.float32), -1), BF16)
    row = lambda i: (i, 0)
    cst = lambda i: (0, 0)
    bat = lambda i: (i // tps, 0, 0)
    return pl.pallas_call(
        _mix_kernel,
        grid=(n // tm,),
        in_specs=[pl.BlockSpec((tm, mw), row), pl.BlockSpec((tm, mw), row),
                  pl.BlockSpec((tm, mw), lambda i: (i, 3)),
                  pl.BlockSpec((1, 1, tm, fw), lambda i: (i // tps, 0, i % tps, 0)),
                  pl.BlockSpec((1, 1, tm, fw), lambda i: (i // tps, 1, i % tps, 0)),
                  pl.BlockSpec((tm, d), row),
                  pl.BlockSpec((1, 1, d), bat), pl.BlockSpec((1, 1, d), bat), pl.BlockSpec((1, 1, d), bat),
                  pl.BlockSpec((1, mw), cst), pl.BlockSpec((1, d), cst),
                  pl.BlockSpec((mw + fw, d), cst),
                  pl.BlockSpec((2 * cg, cg), cst),
                  pl.BlockSpec((N_GROUPS, cg, cg), lambda i: (0, 0, 0)),
                  pl.BlockSpec((d, n_exp), cst), pl.BlockSpec((d, n_exp), cst),
                  pl.BlockSpec((1, n_exp), cst),
                  pl.BlockSpec((tm, tm), cst),
                  pl.BlockSpec((1, n_exp), cst)],
        out_specs=[pl.BlockSpec((tm, d), row), pl.BlockSpec((tm * (d // 2 // LANES), LANES), row),
                   pl.BlockSpec((tm, LANES), row), pl.BlockSpec((tm, LANES), row),
                   pl.BlockSpec((tm, LANES), row), pl.BlockSpec((1, n_exp), cst)],
        out_shape=[jax.ShapeDtypeStruct((n, d), F32), jax.ShapeDtypeStruct((n * (d // 2 // LANES), LANES), jnp.uint32),
                   jax.ShapeDtypeStruct((n, LANES), jnp.int32), jax.ShapeDtypeStruct((n, LANES), F32),
                   jax.ShapeDtypeStruct((n, LANES), F32), jax.ShapeDtypeStruct((1, n_exp), F32)],
        scratch_shapes=[pltpu.VMEM((1, n_exp), F32)],
        compiler_params=_params("arbitrary"),
        name="mix",
    )(hf, hb, proj, pdft, pdft, x2, g1, sc2, sh2, hng.reshape(1, mw), ln2_g.reshape(1, d),
      w_out, mc, w_four, wr_hi, wr_lo, b_router.reshape(1, n_exp), ltri, base0)


def _rows_copy(src, src_row, dst, dst_row, nrows, sem):
    if nrows > 1:
        src_row = pl.multiple_of(src_row, nrows)
        dst_row = pl.multiple_of(dst_row, nrows)
    return pltpu.make_async_copy(src.at[pl.ds(src_row, nrows), :], dst.at[pl.ds(dst_row, nrows), :], sem)


def _for_slot(slot, body):
    for s in (0, 1):
        pl.when(slot == s)(functools.partial(body, s))


def _dispatch_kernel(dest_ref, *refs, tiles, rpt):
    h_refs, (xs_ref, hbuf, sems) = refs[:len(tiles)], refs[len(tiles):]
    tm = hbuf.shape[1] // rpt
    i = pl.program_id(0)
    n_steps = pl.num_programs(0)

    def drain(s):
        def body(t, carry):
            for _ in range(TOP_K):
                _rows_copy(hbuf.at[s], 0, xs_ref, 0, rpt, sems.at[s]).wait()
            return carry
        lax.fori_loop(0, tm, body, 0)

    def step(s):
        @pl.when(i >= 2)
        def _():
            drain(s)

        first = 0
        for h_ref, cnt in zip(h_refs, tiles):
            @pl.when(jnp.logical_and(i >= first, i < first + cnt))
            def _(h_ref=h_ref):
                hbuf[s] = h_ref[...]
            first += cnt

        @pl.when(i >= first)
        def _():
            hbuf[s] = jnp.zeros(hbuf.shape[1:], hbuf.dtype)

        def issue(t, carry):
            for k in range(TOP_K):
                _rows_copy(hbuf.at[s], t * rpt, xs_ref, dest_ref[t * TOP_K + k] * rpt, rpt,
                           sems.at[s]).start(priority=k % 2)
            return carry

        lax.fori_loop(0, tm, issue, 0)

        @pl.when(i == n_steps - 1)
        def _():
            drain(s)

        @pl.when(jnp.logical_and(i == n_steps - 1, i >= 1))
        def _():
            drain(1 - s)

    _for_slot(i % 2, step)


def _dispatch(h2ps, dest_all, rpt):
    tm = ROW_TM
    assert all(h.shape[1] == LANES and h.shape[0] % (tm * rpt) == 0 for h in h2ps)
    tiles = tuple(h.shape[0] // (tm * rpt) for h in h2ps)
    rows = dest_all.shape[0]
    n_steps = rows // (tm * TOP_K)
    in_specs = [pl.BlockSpec((tm * TOP_K,), lambda i: (i,), memory_space=pltpu.SMEM)]
    first = 0
    for cnt in tiles:
        in_specs.append(pl.BlockSpec((tm * rpt, LANES),
                                     lambda i, first=first, cnt=cnt: (jnp.clip(i - first, 0, cnt - 1), 0)))
        first += cnt
    return pl.pallas_call(
        functools.partial(_dispatch_kernel, tiles=tiles, rpt=rpt),
        grid=(n_steps,),
        in_specs=in_specs,
        out_specs=pl.BlockSpec(memory_space=pl.ANY),
        out_shape=jax.ShapeDtypeStruct((rows * rpt, LANES), jnp.uint32),
        scratch_shapes=[pltpu.VMEM((2, tm * rpt, LANES), jnp.uint32), pltpu.SemaphoreType.DMA((2,))],
        compiler_params=_params("arbitrary"),
        name="dispatch",
    )(dest_all, *h2ps)


def _combine_kernel(dcur_ref, dnext_ref, x1_ref, gw_ref, g2_ref, fg_ref, ys_ref, o_ref, buf, sems):
    tm = x1_ref.shape[0]
    rpt = buf.shape[2] // tm
    i = pl.program_id(0)
    n_steps = pl.num_programs(0)

    def issue(d_ref, s):
        def body(t, carry):
            for k in range(TOP_K):
                _rows_copy(ys_ref, d_ref[t * TOP_K + k] * rpt, buf.at[s, k], t * rpt, rpt,
                           sems.at[s]).start(priority=k % 2)
            return carry
        lax.fori_loop(0, tm, body, 0)

    def step(s):
        @pl.when(i == 0)
        def _():
            issue(dcur_ref, s)

        @pl.when(i + 1 < n_steps)
        def _():
            issue(dnext_ref, 1 - s)

        def drain(t, carry):
            for k in range(TOP_K):
                _rows_copy(ys_ref, 0, buf.at[s, k], 0, rpt, sems.at[s]).wait()
            return carry

        lax.fori_loop(0, tm, drain, 0)
        gw = gw_ref[...]
        y = None
        for k in range(TOP_K):
            yk = gw[:, k:k + 1] * _unpack_bf16_pairs(_load_tokens(buf.at[s, k], tm)).astype(F32)
            y = yk if y is None else y + yk
        x2 = x1_ref[...] + g2_ref[0] * y
        o_ref[...] = _rms(x2, fg_ref[...])

    _for_slot(i % 2, step)


def _combine(x1, gw, g2, final_g, ys, dest, seq):
    n, d = x1.shape
    tm = _tile(seq, ROW_TM)
    tps = seq // tm
    last = n // tm - 1
    return pl.pallas_call(
        _combine_kernel,
        grid=(n // tm,),
        in_specs=[pl.BlockSpec((tm * TOP_K,), lambda i: (i,), memory_space=pltpu.SMEM),
                  pl.BlockSpec((tm * TOP_K,), lambda i: (jnp.minimum(i + 1, last),), memory_space=pltpu.SMEM),
                  pl.BlockSpec((tm, d), lambda i: (i, 0)),
                  pl.BlockSpec((tm, LANES), lambda i: (i, 0)),
                  pl.BlockSpec((1, 1, d), lambda i: (i // tps, 0, 0)),
                  pl.BlockSpec((1, d), lambda i: (0, 0)),
                  pl.BlockSpec(memory_space=pl.ANY)],
        out_specs=pl.BlockSpec((tm, d), lambda i: (i, 0)),
        out_shape=jax.ShapeDtypeStruct((n, d), F32),
        scratch_shapes=[pltpu.VMEM((2, TOP_K, tm * (d // 2 // LANES), LANES), jnp.uint32),
                        pltpu.SemaphoreType.DMA((2,))],
        compiler_params=_params("arbitrary"),
        name="combine",
    )(dest, dest, x1, gw, g2, final_g.reshape(1, d), ys)


def _load_tokens(ref, n_tok):
    rpt = ref.shape[0] // n_tok
    return jnp.concatenate([ref[pl.ds(s, n_tok, stride=rpt), :] for s in range(rpt)], axis=1)


def _store_tokens(ref, x):
    n_tok = x.shape[0]
    rpt = x.shape[1] // LANES
    for s in range(rpt):
        ref[pl.ds(s, n_tok, stride=rpt), :] = x[:, s * LANES:(s + 1) * LANES]


def _moe_kernel(be_ref, bv_ref, x_ref, wg_ref, wu_ref, wd_ref, bg_ref, bu_ref, bd_ref, o_ref, xb_scr, acc):
    del be_ref
    valid = bv_ref[pl.program_id(0)] > 0
    f = pl.program_id(1)

    @pl.when(f == 0)
    def _():
        acc[...] = jnp.broadcast_to(bd_ref[0], acc.shape)

    @pl.when(jnp.logical_and(valid, f == 0))
    def _():
        xb_scr[...] = _unpack_bf16_pairs(_load_tokens(x_ref, xb_scr.shape[0]))

    @pl.when(valid)
    def _():
        xb = xb_scr[...]
        gate = jnp.dot(xb, wg_ref[0], preferred_element_type=F32) + bg_ref[0]
        up = jnp.dot(xb, wu_ref[0], preferred_element_type=F32) + bu_ref[0]
        gate = jnp.minimum(gate, SWIGLU_LIMIT)
        up = jnp.clip(up, -SWIGLU_LIMIT, SWIGLU_LIMIT)
        act = (up + 1.0) * (gate * jax.nn.sigmoid(SWIGLU_ALPHA * gate))
        acc[...] += jnp.dot(act.astype(BF16), wd_ref[0], preferred_element_type=F32)

    @pl.when(f == pl.num_programs(1) - 1)
    def _():
        _store_tokens(o_ref, _pack_bf16_pairs(acc[...].astype(BF16)))


def _moe(xs, block_e, block_valid, w_gu, b_gu, w_dn, b_dn):
    n_exp, d_ff, d = w_dn.shape
    rpt = d // 2 // LANES
    rows = xs.shape[0] // rpt
    fc = _tile(d_ff, 1024)
    nf = d_ff // fc
    n_blocks = rows // MOE_BM
    grid_spec = pltpu.PrefetchScalarGridSpec(
        num_scalar_prefetch=2,
        grid=(n_blocks, nf),
        in_specs=[pl.BlockSpec((MOE_BM * rpt, LANES), lambda i, f, be, bv: (i, 0)),
                  pl.BlockSpec((1, d, fc), lambda i, f, be, bv: (be[i], 0, f)),
                  pl.BlockSpec((1, d, fc), lambda i, f, be, bv: (be[i], 0, nf + f)),
                  pl.BlockSpec((1, fc, d), lambda i, f, be, bv: (be[i], f, 0)),
                  pl.BlockSpec((1, 1, fc), lambda i, f, be, bv: (be[i], 0, f)),
                  pl.BlockSpec((1, 1, fc), lambda i, f, be, bv: (be[i], 0, nf + f)),
                  pl.BlockSpec((1, 1, d), lambda i, f, be, bv: (be[i], 0, 0))],
        out_specs=pl.BlockSpec((MOE_BM * rpt, LANES), lambda i, f, be, bv: (i, 0)),
        scratch_shapes=[pltpu.VMEM((MOE_BM, d), BF16), pltpu.VMEM((MOE_BM, d), F32)],
    )
    return pl.pallas_call(
        _moe_kernel,
        grid_spec=grid_spec,
        out_shape=jax.ShapeDtypeStruct((rows * rpt, LANES), jnp.uint32),
        compiler_params=_params("arbitrary", "arbitrary"),
        name="moe",
    )(block_e, block_valid, xs, w_gu, w_gu, w_dn,
      b_gu.reshape(n_exp, 1, 2 * d_ff), b_gu.reshape(n_exp, 1, 2 * d_ff), b_dn.reshape(n_exp, 1, d))


def _mixer_and_router(x, ada, wts, base0):
    (ln1_g, ln2_g, w_main, w_gates, b_gates, conv_w, conv_b, head_norm_g, w_four, w_out_b, mc,
     wr_hi, wr_lo, b_router) = wts
    bsz, seq, d = x.shape
    n = bsz * seq
    mw = d // 2
    dh = mw // N_HEADS
    x2 = x.reshape(n, d)
    sh1, sc1, g1, sh2, sc2, g2 = [a.reshape(bsz, 1, d) for a in jnp.split(ada, 6, axis=-1)]

    proj, gates = _inproj(x2, ln1_g, sc1, sh1, w_main, w_gates, b_gates, seq)
    qk = _conv(proj, conv_w, conv_b, seq, mw, dh ** -0.5)
    gates_t = gates[:, :N_GATES].reshape(bsz, seq, N_GATES).transpose(0, 2, 1).reshape(bsz * N_GATES, seq)
    hf, hb = _mlstm(qk, proj, gates, gates_t, bsz, seq, mw)
    assert d - mw == mw, "the Fourier block is addressed as column block 4 of proj"
    pdft = _position_dft(proj, bsz, seq, mw, 4)
    x1, h2p, idx, gw, rank, cnt = _mix(hf, hb, proj, pdft, x2, g1, sc2, sh2, head_norm_g, ln2_g,
                                       w_out_b, mc, w_four, wr_hi, wr_lo, b_router, base0, seq)
    return dict(x1=x1, h2p=h2p, idx=idx[:, :TOP_K], gw=gw, rank=rank[:, :TOP_K].astype(jnp.int32),
                cnt=cnt, g2=g2, shape=(bsz, seq, d))


def kernel(x_prompt, x_sample, c_prompt, c_sample, ln1_g, ln2_g, w_ada, b_ada, w_in, b_gates, conv_w,
           conv_b, head_norm_g, w_four, w_out, w_router, b_router, w_gate_up, b_gate_up, w_down, b_down,
           final_g):
    assert w_ada.shape[0] == 1, "single-layer trunk"
    d = x_prompt.shape[-1]
    mw = d // 2
    cg = (d - mw) // N_GROUPS
    w_in0 = w_in[0]
    w_main = jnp.concatenate([w_in0[:, :4 * mw], w_in0[:, 4 * mw + N_GATES:]], axis=1).astype(BF16)
    w_gates = jnp.pad(w_in0[:, 4 * mw:4 * mw + N_GATES], ((0, 0), (0, LANES - N_GATES))).astype(BF16)
    bg = jnp.pad(b_gates[0], (0, LANES - N_GATES)).reshape(1, LANES)
    kc = np.arange(cg)
    angc = 2.0 * np.pi * np.outer(kc, kc) / cg
    mc = jnp.asarray(np.concatenate([np.cos(angc), -np.sin(angc)], axis=0) / math.sqrt(cg), BF16)
    wr = w_router[0]
    wr_hi = wr.astype(BF16)
    wr_lo = (wr - wr_hi.astype(F32)).astype(BF16)
    wts = (ln1_g[0], ln2_g[0], w_main, w_gates, bg, conv_w[0], conv_b[0], head_norm_g[0], w_four[0],
           w_out[0].astype(BF16), mc, wr_hi, wr_lo, b_router[0])
    n_exp = wr.shape[1]
    nbp = c_prompt.shape[0]
    ada = _ada(jnp.concatenate([c_prompt, c_sample], axis=0), w_ada[0], b_ada[0])
    gp = _mixer_and_router(x_prompt, ada[:nbp], wts, jnp.zeros((1, n_exp), F32))
    gs = _mixer_and_router(x_sample, ada[nbp:], wts, gp["cnt"])
    groups = (gp, gs)

    counts = gs["cnt"][0].astype(jnp.int32)
    p_counts = (counts + MOE_BM - 1) // MOE_BM * MOE_BM
    p_ends = jnp.cumsum(p_counts)
    p_off = p_ends - p_counts
    n_tok = sum(g["x1"].shape[0] for g in groups)
    n_blocks = (n_tok * TOP_K) // MOE_BM + n_exp
    starts = jnp.arange(n_blocks, dtype=jnp.int32) * MOE_BM
    block_e = jnp.minimum(jnp.sum((starts[:, None] >= p_ends[None, :]).astype(jnp.int32), axis=1), n_exp - 1)
    block_valid = (starts < p_ends[-1]).astype(jnp.int32)
    dests = [(p_off[g["idx"]] + g["rank"]).reshape(-1) for g in groups]
    rows = n_blocks * MOE_BM
    gap_start = jnp.concatenate([p_off + counts, p_ends[-1:]])
    gap_len = jnp.concatenate([p_counts - counts, rows - p_ends[-1:]])
    gap_end = jnp.cumsum(gap_len)
    j = jnp.arange(n_exp * MOE_BM, dtype=jnp.int32)
    gi = jnp.sum((j[:, None] >= gap_end[None, :]).astype(jnp.int32), axis=1)
    pad_dest = (gap_start[gi] + (j - (gap_end - gap_len)[gi])).astype(jnp.int32)

    xs = _dispatch([g["h2p"] for g in groups], jnp.concatenate(dests + [pad_dest]), d // 2 // LANES)
    ys = _moe(xs, block_e, block_valid, w_gate_up[0].astype(BF16), b_gate_up[0], w_down[0].astype(BF16), b_down[0])
    outs = []
    for g, dest in zip(groups, dests):
        bsz, seq, _ = g["shape"]
        outs.append(_combine(g["x1"], g["gw"], g["g2"], final_g, ys, dest, seq).reshape(g["shape"]))
    return tuple(outs)
```

```python
import functools
import math

import numpy as np
import jax
import jax.numpy as jnp
from jax import lax
from jax.experimental import pallas as pl
from jax.experimental.pallas import tpu as pltpu

F32 = jnp.float32
BF16 = jnp.bfloat16

N_HEADS = 4
N_GROUPS = 4
N_GATES = 4 * N_HEADS
CHUNK = 128
TOP_K = 4
SWIGLU_LIMIT = 7.0
SWIGLU_ALPHA = 1.702
EPS = 1e-6
LANES = 128
SUBLANES = 8
DFT_N2 = 128
MOE_BM = 512
ROW_TM = 256
VMEM_LIMIT = 56 * 1024 * 1024


def _params(*sem, flags=None):
    return pltpu.CompilerParams(dimension_semantics=sem, vmem_limit_bytes=VMEM_LIMIT, flags=flags)


def _tile(n, pref):
    t = min(n, pref)
    while n % t:
        t //= 2
    return t


def _ada_kernel(c_ref, w_ref, b_ref, o_ref):
    c = c_ref[...]
    s = (c * jax.nn.sigmoid(c)).astype(BF16)
    o_ref[...] = jnp.dot(s, w_ref[...].astype(BF16), preferred_element_type=F32) + b_ref[...]


def _ada(c, w_ada, b_ada):
    bsz, d = c.shape
    bp = -(-bsz // SUBLANES) * SUBLANES
    cp = jnp.pad(c, ((0, bp - bsz), (0, 0)))
    n_out = w_ada.shape[1]
    tn = _tile(n_out, 1024)
    out = pl.pallas_call(
        _ada_kernel,
        grid=(n_out // tn,),
        in_specs=[pl.BlockSpec((bp, d), lambda j: (0, 0)),
                  pl.BlockSpec((d, tn), lambda j: (0, j)),
                  pl.BlockSpec((1, tn), lambda j: (0, j))],
        out_specs=pl.BlockSpec((bp, tn), lambda j: (0, j)),
        out_shape=jax.ShapeDtypeStruct((bp, n_out), F32),
        compiler_params=_params("arbitrary"),
        name="ada",
    )(cp, w_ada, b_ada.reshape(1, n_out))
    return out[:bsz]


def _rms(x, g):
    return (x * lax.rsqrt(jnp.mean(x * x, axis=-1, keepdims=True) + EPS)) * g


def _inproj_kernel(x_ref, g_ref, sc_ref, sh_ref, w_ref, wg_ref, bg_ref, o_ref, og_ref, h_scr):
    @pl.when(pl.program_id(1) == 0)
    def _():
        h = _rms(x_ref[...], g_ref[...]) * (1.0 + sc_ref[0]) + sh_ref[0]
        hb = h.astype(BF16)
        h_scr[...] = hb
        og_ref[...] = jnp.dot(hb, wg_ref[...], preferred_element_type=F32) + bg_ref[...]

    o_ref[...] = jnp.dot(h_scr[...], w_ref[...], preferred_element_type=F32)


def _inproj(x2, ln_g, sc, sh, w_main, w_gates, b_gates, seq):
    n, d = x2.shape
    p = w_main.shape[1]
    tm = _tile(seq, 1024)
    tn = _tile(p, 1024)
    tps = seq // tm
    return pl.pallas_call(
        _inproj_kernel,
        grid=(n // tm, p // tn),
        in_specs=[pl.BlockSpec((tm, d), lambda i, j: (i, 0)),
                  pl.BlockSpec((1, d), lambda i, j: (0, 0)),
                  pl.BlockSpec((1, 1, d), lambda i, j: (i // tps, 0, 0)),
                  pl.BlockSpec((1, 1, d), lambda i, j: (i // tps, 0, 0)),
                  pl.BlockSpec((d, tn), lambda i, j: (0, j)),
                  pl.BlockSpec((d, LANES), lambda i, j: (0, 0)),
                  pl.BlockSpec((1, LANES), lambda i, j: (0, 0))],
        out_specs=[pl.BlockSpec((tm, tn), lambda i, j: (i, j)),
                   pl.BlockSpec((tm, LANES), lambda i, j: (i, 0))],
        out_shape=[jax.ShapeDtypeStruct((n, p), F32),
                   jax.ShapeDtypeStruct((n, LANES), F32)],
        scratch_shapes=[pltpu.VMEM((tm, d), BF16)],
        compiler_params=_params("arbitrary", "arbitrary"),
        name="inproj",
    )(x2, ln_g.reshape(1, d), sc, sh, w_main, w_gates, b_gates)


def _conv_kernel(x_ref, prev_ref, next_ref, w_ref, b_ref, q_ref, kt_ref, res_scr, *, tps, k_scale):
    it = pl.program_id(0) % tps
    w = w_ref[...]
    b = b_ref[...]
    taps = w.shape[0]
    half = taps // 2

    def conv_act(z):
        nz = z.shape[0]
        acc = z * w[half:half + 1]
        for j in range(taps):
            if j != half:
                acc = acc + pltpu.roll(z, (half - j) % nz, 0) * w[j:j + 1]
        acc = acc + b
        return acc * jax.nn.sigmoid(acc)

    x = x_ref[...]
    tm = x.shape[0]
    res_scr[...] = conv_act(x)
    prev = jnp.where(it == 0, 0.0, prev_ref[...])
    nxt = jnp.where(it == tps - 1, 0.0, next_ref[...])
    top = conv_act(jnp.concatenate([prev, x[0:2 * SUBLANES]], axis=0))
    res_scr[0:SUBLANES, :] = top[SUBLANES:2 * SUBLANES]
    bot = conv_act(jnp.concatenate([x[tm - 2 * SUBLANES:tm], nxt], axis=0))
    res_scr[tm - SUBLANES:tm, :] = bot[SUBLANES:2 * SUBLANES]

    @pl.when(pl.program_id(1) == 0)
    def _():
        q_ref[...] = res_scr[...]

    @pl.when(pl.program_id(1) == 1)
    def _():
        for r in range(0, tm, LANES):
            kt_ref[:, r:r + LANES] = (res_scr[r:r + LANES, :] * k_scale).T


def _conv(proj, conv_w, conv_b, seq, mw, k_scale):
    n = proj.shape[0]
    taps = conv_w.shape[0]
    assert taps // 2 <= SUBLANES
    tm = _tile(seq, 512)
    assert tm >= 4 * SUBLANES
    tps = seq // tm
    r8 = tm // SUBLANES
    last8 = n // SUBLANES - 1
    return pl.pallas_call(
        functools.partial(_conv_kernel, tps=tps, k_scale=k_scale),
        grid=(n // tm, 2),
        in_specs=[pl.BlockSpec((tm, mw), lambda i, j: (i, j)),
                  pl.BlockSpec((SUBLANES, mw), lambda i, j: (jnp.maximum(i * r8 - 1, 0), j)),
                  pl.BlockSpec((SUBLANES, mw), lambda i, j: (jnp.minimum((i + 1) * r8, last8), j)),
                  pl.BlockSpec((taps, mw), lambda i, j: (0, j)),
                  pl.BlockSpec((1, mw), lambda i, j: (0, j))],
        out_specs=[pl.BlockSpec((tm, mw), lambda i, j: (i, 0)),
                   pl.BlockSpec((mw, tm), lambda i, j: (0, i))],
        out_shape=[jax.ShapeDtypeStruct((n, mw), F32), jax.ShapeDtypeStruct((mw, n), F32)],
        scratch_shapes=[pltpu.VMEM((tm, mw), F32)],
        compiler_params=_params("arbitrary", "arbitrary"),
        name="conv",
    )(proj, proj, proj, conv_w, conv_b.reshape(1, 2 * mw))


def _log_sigmoid(x):
    return jnp.minimum(x, 0.0) - jnp.log1p(jnp.exp(-jnp.abs(x)))


def _split3(x):
    hi = x.astype(BF16)
    r = x - hi.astype(F32)
    mid = r.astype(BF16)
    lo = (r - mid.astype(F32)).astype(BF16)
    return hi, mid, lo


def _cummax(x, axis, reverse):
    n = x.shape[axis]
    pos = lax.broadcasted_iota(jnp.int32, x.shape, axis)
    step = 1
    while step < n:
        if reverse:
            shifted = jnp.where(pos < n - step, pltpu.roll(x, n - step, axis), -jnp.inf)
        else:
            shifted = jnp.where(pos >= step, pltpu.roll(x, step, axis), -jnp.inf)
        x = jnp.maximum(x, shifted)
        step *= 2
    return x


def _mlstm_gates(gc, gr, tri_c, tri_r, m_lane, m_sub, reverse):
    length = gc.shape[0]
    last = 0 if reverse else length - 1
    b_cols = jnp.dot(tri_c, jnp.concatenate(_split3(_log_sigmoid(gc)), axis=0), preferred_element_type=F32)
    b_rows = jnp.dot(jnp.concatenate(_split3(_log_sigmoid(gr)), axis=1), tri_r, preferred_element_type=F32)
    b_cols = pltpu.roll(b_cols, LANES - N_HEADS, 1)
    b_rows = pltpu.roll(b_rows, N_GATES - N_HEADS, 0)
    u_cols = gc - b_cols
    u_rows = gr - b_rows
    c_cols = jnp.maximum(m_lane, _cummax(u_cols, 0, reverse))
    c_rows = jnp.maximum(m_sub, _cummax(u_rows, 1, reverse))
    c_last = c_rows[:, last:last + 1]
    out = dict(
        c_cols=c_cols, u_rows=u_rows,
        ea_cols=jnp.exp(m_lane - c_cols),
        floor_cols=jnp.exp(-b_cols - c_cols),
        ew_rows=jnp.exp(u_rows - c_last),
        decay_rows=jnp.exp(m_sub - c_last),
        m_lane=b_cols[last:last + 1, :] + c_cols[last:last + 1, :],
        m_sub=jnp.broadcast_to(b_rows[:, last:last + 1] + c_last, m_sub.shape),
    )
    return out


def _mlstm_heads(work, gate_fn, cn_ref):
    length = work[0][0].shape[0]
    row = lax.broadcasted_iota(jnp.int32, (length, length), 0)
    col = lax.broadcasted_iota(jnp.int32, (length, length), 1)
    qxs = []
    for q_ref, kt_ref, v_ref, h_ref, sl, di, ci, idx, reverse in work:
        rhs = jnp.concatenate([kt_ref[sl, :].astype(BF16), cn_ref[idx].astype(BF16)], axis=1)
        qxs.append(jnp.dot(q_ref[:, sl].astype(BF16), rhs, preferred_element_type=F32))
    gts = gate_fn()
    work = [w[:5] + (gts[w[5]],) + w[6:] for w in work]
    wms, lhs, wvs = [], [], []
    for (q_ref, kt_ref, v_ref, h_ref, sl, gt, ci, idx, reverse), qx in zip(work, qxs):
        mask = (col >= row) if reverse else (col <= row)
        wm = jnp.exp(jnp.where(mask, gt["u_rows"][ci:ci + 1, :] - gt["c_cols"][:, ci:ci + 1], -jnp.inf))
        wm = wm * qx[:, :length]
        kw = kt_ref[sl, :] * gt["ew_rows"][ci:ci + 1, :]
        wms.append(wm)
        lhs.append(jnp.concatenate([wm, kw], axis=0).astype(BF16))
    for (q_ref, kt_ref, v_ref, h_ref, sl, gt, ci, idx, reverse), lh in zip(work, lhs):
        v1 = jnp.concatenate([v_ref[:, sl].astype(BF16), jnp.ones((length, LANES), BF16)], axis=1)
        wvs.append(jnp.dot(lh, v1, preferred_element_type=F32))
    for (q_ref, kt_ref, v_ref, h_ref, sl, gt, ci, idx, reverse), qx, wm, wv in zip(work, qxs, wms, wvs):
        dh = wv.shape[1] - LANES
        ea = gt["ea_cols"][:, ci:ci + 1]
        num = wv[:length, :dh] + ea * qx[:, length:length + dh]
        den = jnp.sum(wm, axis=1, keepdims=True) + ea * qx[:, length + dh:length + dh + 1]
        h_ref[:, sl] = num / jnp.maximum(jnp.abs(den), gt["floor_cols"][:, ci:ci + 1])
        width = wv.shape[1]
        decay = jnp.concatenate([gt["decay_rows"][ci:ci + 1, :]] * pl.cdiv(width, LANES), axis=1)[:, :width]
        cn_ref[idx] = decay * cn_ref[idx] + wv[length:, :]


def _mlstm_kernel(qf_ref, kf_ref, vf_ref, gcf_ref, grf_ref,
                  qb_ref, kb_ref, vb_ref, gcb_ref, grb_ref,
                  l3_ref, u3_ref, l3t_ref, u3t_ref,
                  hf_ref, hb_ref, cn_scr, ml_scr, ms_scr):
    @pl.when(pl.program_id(1) == 0)
    def _():
        cn_scr[...] = jnp.zeros_like(cn_scr)
        ml_scr[...] = jnp.zeros_like(ml_scr)
        ms_scr[...] = jnp.zeros_like(ms_scr)

    dh = qf_ref.shape[1] // N_HEADS
    dirs = (
        (False, qf_ref, kf_ref, vf_ref, gcf_ref, grf_ref, l3_ref, u3t_ref, hf_ref),
        (True, qb_ref, kb_ref, vb_ref, gcb_ref, grb_ref, u3_ref, l3t_ref, hb_ref),
    )
    work = []
    for di, (reverse, q_ref, kt_ref, v_ref, gc_ref, gr_ref, tri_c, tri_r, h_ref) in enumerate(dirs):
        for hd in range(N_HEADS):
            sl = slice(hd * dh, (hd + 1) * dh)
            work.append((q_ref, kt_ref, v_ref, h_ref, sl, di, di * 2 * N_HEADS + hd, di * N_HEADS + hd, reverse))

    def gate_fn():
        gts = []
        for di, (reverse, q_ref, kt_ref, v_ref, gc_ref, gr_ref, tri_c, tri_r, h_ref) in enumerate(dirs):
            gt = _mlstm_gates(gc_ref[...], gr_ref[...], tri_c[...], tri_r[...], ml_scr[di], ms_scr[di], reverse)
            ml_scr[di] = gt["m_lane"]
            ms_scr[di] = gt["m_sub"]
            gts.append(gt)
        return gts

    _mlstm_heads(work, gate_fn, cn_scr)


def _tri_consts(length):
    lower = np.tril(np.ones((length, length), np.float32))
    upper = lower.T
    l3 = np.concatenate([lower] * 3, axis=1)
    u3 = np.concatenate([upper] * 3, axis=1)
    l3t = np.concatenate([lower] * 3, axis=0)
    u3t = np.concatenate([upper] * 3, axis=0)
    return tuple(jnp.asarray(a, BF16) for a in (l3, u3, l3t, u3t))


def _mlstm(q, kt, proj, gates, gates_t, bsz, seq, mw):
    n = q.shape[0]
    nc = seq // CHUNK
    dh = mw // N_HEADS
    assert CHUNK == LANES
    fwd = lambda b, c: b * nc + c
    bwd = lambda b, c: b * nc + (nc - 1 - c)
    cst = lambda b, c: (0, 0)
    in_specs = []
    for pos in (fwd, bwd):
        in_specs += [
            pl.BlockSpec((CHUNK, mw), lambda b, c, pos=pos: (pos(b, c), 0)),
            pl.BlockSpec((mw, CHUNK), lambda b, c, pos=pos: (0, pos(b, c))),
            pl.BlockSpec((CHUNK, mw), lambda b, c, pos=pos: (pos(b, c), 2)),
            pl.BlockSpec((CHUNK, LANES), lambda b, c, pos=pos: (pos(b, c), 0)),
            pl.BlockSpec((N_GATES, CHUNK), lambda b, c, pos=pos: (b, pos(0, c))),
        ]
    in_specs += [pl.BlockSpec((CHUNK, 3 * CHUNK), cst), pl.BlockSpec((CHUNK, 3 * CHUNK), cst),
                 pl.BlockSpec((3 * CHUNK, CHUNK), cst), pl.BlockSpec((3 * CHUNK, CHUNK), cst)]
    return pl.pallas_call(
        _mlstm_kernel,
        grid=(bsz, nc),
        in_specs=in_specs,
        out_specs=[pl.BlockSpec((CHUNK, mw), lambda b, c: (fwd(b, c), 0)),
                   pl.BlockSpec((CHUNK, mw), lambda b, c: (bwd(b, c), 0))],
        out_shape=[jax.ShapeDtypeStruct((n, mw), F32), jax.ShapeDtypeStruct((n, mw), F32)],
        scratch_shapes=[pltpu.VMEM((2 * N_HEADS, dh, dh + LANES), F32),
                        pltpu.VMEM((2, 1, LANES), F32),
                        pltpu.VMEM((2, N_GATES, LANES), F32)],
        compiler_params=_params("arbitrary", "arbitrary"),
        name="mlstm",
    )(q, kt, proj, gates, gates_t, q, kt, proj, gates, gates_t, *_tri_consts(CHUNK))


def _dft_a_kernel(x_ref, ga_ref, tc_ref, ts_ref, o_ref, *, jb):
    n1, width = x_ref.shape[1], x_ref.shape[3]
    rows = n1 * SUBLANES
    reps = width // LANES
    ga = ga_ref[...]
    for s in range(jb):
        sl = slice(s * SUBLANES, (s + 1) * SUBLANES)
        x = x_ref[0, :, sl, :].reshape(rows, width).astype(BF16)
        z = jnp.dot(ga, x, preferred_element_type=F32)
        zc, zs = z[:rows], z[rows:]
        tc = jnp.concatenate([tc_ref[s]] * reps, axis=1)
        ts = jnp.concatenate([ts_ref[s]] * reps, axis=1)
        o_ref[0, 0, :, sl, :] = (zc * tc - zs * ts).reshape(n1, SUBLANES, width)
        o_ref[0, 1, :, sl, :] = (zc * ts + zs * tc).reshape(n1, SUBLANES, width)


def _dft_b_kernel(z_ref, gb_ref, o_ref, scr):
    n2, wb = z_ref.shape[3], z_ref.shape[4]
    gb = gb_ref[...]
    for k in range(SUBLANES):
        slab = z_ref[0, :, k].reshape(2 * n2, wb).astype(BF16)
        p = jnp.dot(gb, slab, preferred_element_type=F32)
        for cs in range(2):
            for c in range(wb // LANES):
                scr[c, pl.ds(cs * n2 * SUBLANES + k, n2, stride=SUBLANES), :] = (
                    p[cs * n2:(cs + 1) * n2, c * LANES:(c + 1) * LANES])
    for c in range(wb // LANES):
        o_ref[0, :, :, :, c * LANES:(c + 1) * LANES] = scr[c].reshape(2, n2, SUBLANES, LANES)


def _position_dft(proj, bsz, seq, width, col_block):
    n2 = DFT_N2
    n1 = seq // n2
    assert seq % n2 == 0 and n1 % SUBLANES == 0 and proj.shape[1] % width == 0
    rows = n1 * SUBLANES
    k1 = np.arange(n1)
    ang1 = 2.0 * np.pi * np.outer(k1, k1) / n1
    eye = np.eye(SUBLANES)
    ga = jnp.asarray(np.concatenate([np.kron(np.cos(ang1), eye), np.kron(np.sin(ang1), eye)], axis=0)
                     / math.sqrt(n1), BF16)
    k2 = np.arange(n2)
    ang2 = 2.0 * np.pi * np.outer(k2, k2) / n2
    c2, s2 = np.cos(ang2), np.sin(ang2)
    gb = jnp.asarray(np.block([[c2, -s2], [s2, c2]]) / math.sqrt(n2), BF16)
    n2_idx = jnp.arange(n2, dtype=jnp.int32).reshape(n2 // SUBLANES, 1, SUBLANES)
    k1_idx = jnp.arange(n1, dtype=jnp.int32).reshape(1, n1, 1)
    ang = ((2.0 * math.pi / seq) * ((n2_idx * k1_idx) % seq).astype(F32)).reshape(n2 // SUBLANES, rows)
    tc = jnp.broadcast_to(jnp.cos(ang)[:, :, None], (n2 // SUBLANES, rows, LANES))
    ts = jnp.broadcast_to(jnp.sin(ang)[:, :, None], (n2 // SUBLANES, rows, LANES))

    jb = max(1, min(n2 // SUBLANES, 512 // rows))
    proj4 = proj.reshape(bsz, n1, n2, proj.shape[1])
    z = pl.pallas_call(
        functools.partial(_dft_a_kernel, jb=jb),
        grid=(bsz, n2 // (SUBLANES * jb)),
        in_specs=[pl.BlockSpec((1, n1, SUBLANES * jb, width), lambda b, j: (b, 0, j, col_block)),
                  pl.BlockSpec((2 * rows, rows), lambda b, j: (0, 0)),
                  pl.BlockSpec((jb, rows, LANES), lambda b, j: (j, 0, 0)),
                  pl.BlockSpec((jb, rows, LANES), lambda b, j: (j, 0, 0))],
        out_specs=pl.BlockSpec((1, 2, n1, SUBLANES * jb, width), lambda b, j: (b, 0, 0, j, 0)),
        out_shape=jax.ShapeDtypeStruct((bsz, 2, n1, n2, width), F32),
        compiler_params=_params("arbitrary", "arbitrary"),
        name="dft_a",
    )(proj4, ga, tc, ts)
    wb = _tile(width, 512)
    p = pl.pallas_call(
        _dft_b_kernel,
        grid=(bsz, n1 // SUBLANES, width // wb),
        in_specs=[pl.BlockSpec((1, 2, SUBLANES, n2, wb), lambda b, j, w: (b, 0, j, 0, w)),
                  pl.BlockSpec((2 * n2, 2 * n2), lambda b, j, w: (0, 0))],
        out_specs=pl.BlockSpec((1, 2, n2, SUBLANES, wb), lambda b, j, w: (b, 0, 0, j, w)),
        out_shape=jax.ShapeDtypeStruct((bsz, 2, n2, n1, width), F32),
        scratch_shapes=[pltpu.VMEM((wb // LANES, 2 * n2 * SUBLANES, LANES), F32)],
        compiler_params=_params("arbitrary", "arbitrary", "arbitrary"),
        name="dft_b",
    )(z, gb)
    return p.reshape(bsz, 2, seq, width)


def _pack_bf16_pairs(xb):
    c = xb.shape[1] // 2
    lo = pltpu.bitcast(xb[:, :c].astype(F32), jnp.uint32)
    hi = pltpu.bitcast(xb[:, c:].astype(F32), jnp.uint32)
    return (lo >> 16) | hi


def _unpack_bf16_pairs(w):
    lo = pltpu.bitcast(w << 16, F32).astype(BF16)
    hi = pltpu.bitcast(w & jnp.uint32(0xFFFF0000), F32).astype(BF16)
    return jnp.concatenate([lo, hi], axis=1)


def _mix_kernel(hf_ref, hb_ref, op_ref, pc_ref, ps_ref, x_ref, g1_ref, sc_ref, sh_ref,
                hng_ref, ln_ref, wout_ref, mc_ref, wf_ref, wrh_ref, wrl_ref, br_ref, ltri_ref, base0_ref,
                x1_ref, h2_ref, idx_ref, gw_ref, rank_ref, cnt_ref, base_scr):
    @pl.when(pl.program_id(0) == 0)
    def _():
        base_scr[...] = base0_ref[...]

    tm, mw = hf_ref.shape
    dh = mw // N_HEADS
    hm = hf_ref[...] + hb_ref[...]
    parts = []
    for hd in range(N_HEADS):
        seg = hm[:, hd * dh:(hd + 1) * dh]
        parts.append(seg * lax.rsqrt(jnp.mean(seg * seg, axis=-1, keepdims=True) + EPS))
    hm = jnp.concatenate(parts, axis=1) * hng_ref[...] * jax.nn.sigmoid(op_ref[...])
    cg = pc_ref.shape[3] // N_GROUPS
    pc = pc_ref[0, 0]
    ps = ps_ref[0, 0]
    for gi in range(N_GROUPS):
        sl = slice(gi * cg, (gi + 1) * cg)
        spec = jnp.dot(jnp.concatenate([pc[:, sl], ps[:, sl]], axis=1).astype(BF16), mc_ref[...],
                       preferred_element_type=F32)
        parts.append(jnp.dot(spec.astype(BF16), wf_ref[gi].astype(BF16), preferred_element_type=F32))
    cat = jnp.concatenate([hm] + parts[N_HEADS:], axis=1).astype(BF16)
    mix = jnp.dot(cat, wout_ref[...], preferred_element_type=F32)
    x1 = x_ref[...] + g1_ref[0] * mix
    x1_ref[...] = x1
    h2 = _rms(x1, ln_ref[...]) * (1.0 + sc_ref[0]) + sh_ref[0]
    hh = h2.astype(BF16)
    _store_tokens(h2_ref, _pack_bf16_pairs(hh))

    hl = (h2 - hh.astype(F32)).astype(BF16)
    wrh = wrh_ref[...]
    logits = (jnp.dot(hh, wrh, preferred_element_type=F32)
              + jnp.dot(hl, wrh, preferred_element_type=F32)
              + jnp.dot(hh, wrl_ref[...], preferred_element_type=F32)) + br_ref[...]
    n_exp = logits.shape[1]
    lane = lax.broadcasted_iota(jnp.int32, (tm, n_exp), 1).astype(F32)
    lane_o = lax.broadcasted_iota(jnp.int32, (tm, LANES), 1)
    base = base_scr[...]
    ltri = ltri_ref[...]
    idx_out = jnp.zeros((tm, LANES), F32)
    val_out = jnp.zeros((tm, LANES), F32)
    rank_out = jnp.zeros((tm, LANES), F32)
    top0 = None
    denom = jnp.zeros((tm, 1), F32)
    work = logits
    for k in range(TOP_K):
        mx = jnp.max(work, axis=1, keepdims=True)
        ik = jnp.min(jnp.where(work == mx, lane, float(n_exp)), axis=1, keepdims=True)
        hit = lane == ik
        work = jnp.where(hit, -jnp.inf, work)
        if top0 is None:
            top0 = mx
        ek = jnp.exp(mx - top0)
        denom = denom + ek
        onehot = hit.astype(F32)
        before = jnp.dot(ltri, onehot.astype(BF16), preferred_element_type=F32)
        rk = jnp.sum(jnp.where(hit, base + before, 0.0), axis=1, keepdims=True)
        base = base + jnp.sum(onehot, axis=0, keepdims=True)
        idx_out = jnp.where(lane_o == k, ik, idx_out)
        val_out = jnp.where(lane_o == k, ek, val_out)
        rank_out = jnp.where(lane_o == k, rk, rank_out)
    idx_ref[...] = idx_out.astype(jnp.int32)
    gw_ref[...] = val_out / denom
    rank_ref[...] = rank_out
    base_scr[...] = base
    cnt_ref[...] = base


def _mix(hf, hb, proj, pdft, x2, g1, sc2, sh2, hng, ln2_g, w_out, mc, w_four, wr_hi, wr_lo, b_router,
         base0, seq):
    n, d = x2.shape
    mw = hf.shape[1]
    fw = pdft.shape[3]
    n_exp = wr_hi.shape[1]
    cg = fw // N_GROUPS
    tm = _tile(seq, 256)
    tps = seq // tm
    ltri = jnp.asarray(np.tril(np.ones((tm, tm), np.float32), -1), BF16)
    row = lambda i: (i, 0)
    cst = lambda i: (0, 0)
    bat = lambda i: (i // tps, 0, 0)
    return pl.pallas_call(
        _mix_kernel,
        grid=(n // tm,),
        in_specs=[pl.BlockSpec((tm, mw), row), pl.BlockSpec((tm, mw), row),
                  pl.BlockSpec((tm, mw), lambda i: (i, 3)),
                  pl.BlockSpec((1, 1, tm, fw), lambda i: (i // tps, 0, i % tps, 0)),
                  pl.BlockSpec((1, 1, tm, fw), lambda i: (i // tps, 1, i % tps, 0)),
                  pl.BlockSpec((tm, d), row),
                  pl.BlockSpec((1, 1, d), bat), pl.BlockSpec((1, 1, d), bat), pl.BlockSpec((1, 1, d), bat),
                  pl.BlockSpec((1, mw), cst), pl.BlockSpec((1, d), cst),
                  pl.BlockSpec((mw + fw, d), cst),
                  pl.BlockSpec((2 * cg, cg), cst),
                  pl.BlockSpec((N_GROUPS, cg, cg), lambda i: (0, 0, 0)),
                  pl.BlockSpec((d, n_exp), cst), pl.BlockSpec((d, n_exp), cst),
                  pl.BlockSpec((1, n_exp), cst),
                  pl.BlockSpec((tm, tm), cst),
                  pl.BlockSpec((1, n_exp), cst)],
        out_specs=[pl.BlockSpec((tm, d), row), pl.BlockSpec((tm * (d // 2 // LANES), LANES), row),
                   pl.BlockSpec((tm, LANES), row), pl.BlockSpec((tm, LANES), row),
                   pl.BlockSpec((tm, LANES), row), pl.BlockSpec((1, n_exp), cst)],
        out_shape=[jax.ShapeDtypeStruct((n, d), F32), jax.ShapeDtypeStruct((n * (d // 2 // LANES), LANES), jnp.uint32),
                   jax.ShapeDtypeStruct((n, LANES), jnp.int32), jax.ShapeDtypeStruct((n, LANES), F32),
                   jax.ShapeDtypeStruct((n, LANES), F32), jax.ShapeDtypeStruct((1, n_exp), F32)],
        scratch_shapes=[pltpu.VMEM((1, n_exp), F32)],
        compiler_params=_params("arbitrary"),
        name="mix",
    )(hf, hb, proj, pdft, pdft, x2, g1, sc2, sh2, hng.reshape(1, mw), ln2_g.reshape(1, d),
      w_out, mc, w_four, wr_hi, wr_lo, b_router.reshape(1, n_exp), ltri, base0)


def _rows_copy(src, src_row, dst, dst_row, nrows, sem):
    if nrows > 1:
        src_row = pl.multiple_of(src_row, nrows)
        dst_row = pl.multiple_of(dst_row, nrows)
    return pltpu.make_async_copy(src.at[pl.ds(src_row, nrows), :], dst.at[pl.ds(dst_row, nrows), :], sem)


def _for_slot(slot, body):
    for s in (0, 1):
        pl.when(slot == s)(functools.partial(body, s))


def _dispatch_kernel(dest_ref, *refs, tiles, rpt):
    h_refs, (xs_ref, hbuf, sems) = refs[:len(tiles)], refs[len(tiles):]
    tm = hbuf.shape[1] // rpt
    i = pl.program_id(0)
    n_steps = pl.num_programs(0)

    def drain(s):
        def body(t, carry):
            for _ in range(TOP_K):
                _rows_copy(hbuf.at[s], 0, xs_ref, 0, rpt, sems.at[s]).wait()
            return carry
        lax.fori_loop(0, tm, body, 0)

    def step(s):
        @pl.when(i >= 2)
        def _():
            drain(s)

        first = 0
        for h_ref, cnt in zip(h_refs, tiles):
            @pl.when(jnp.logical_and(i >= first, i < first + cnt))
            def _(h_ref=h_ref):
                hbuf[s] = h_ref[...]
            first += cnt

        @pl.when(i >= first)
        def _():
            hbuf[s] = jnp.zeros(hbuf.shape[1:], hbuf.dtype)

        def issue(t, carry):
            for k in range(TOP_K):
                _rows_copy(hbuf.at[s], t * rpt, xs_ref, dest_ref[t * TOP_K + k] * rpt, rpt,
                           sems.at[s]).start(priority=k % 2)
            return carry

        lax.fori_loop(0, tm, issue, 0)

        @pl.when(i == n_steps - 1)
        def _():
            drain(s)

        @pl.when(jnp.logical_and(i == n_steps - 1, i >= 1))
        def _():
            drain(1 - s)

    _for_slot(i % 2, step)


def _dispatch(h2ps, dest_all, rpt):
    tm = ROW_TM
    assert all(h.shape[1] == LANES and h.shape[0] % (tm * rpt) == 0 for h in h2ps)
    tiles = tuple(h.shape[0] // (tm * rpt) for h in h2ps)
    rows = dest_all.shape[0]
    n_steps = rows // (tm * TOP_K)
    in_specs = [pl.BlockSpec((tm * TOP_K,), lambda i: (i,), memory_space=pltpu.SMEM)]
    first = 0
    for cnt in tiles:
        in_specs.append(pl.BlockSpec((tm * rpt, LANES),
                                     lambda i, first=first, cnt=cnt: (jnp.clip(i - first, 0, cnt - 1), 0)))
        first += cnt
    return pl.pallas_call(
        functools.partial(_dispatch_kernel, tiles=tiles, rpt=rpt),
        grid=(n_steps,),
        in_specs=in_specs,
        out_specs=pl.BlockSpec(memory_space=pl.ANY),
        out_shape=jax.ShapeDtypeStruct((rows * rpt, LANES), jnp.uint32),
        scratch_shapes=[pltpu.VMEM((2, tm * rpt, LANES), jnp.uint32), pltpu.SemaphoreType.DMA((2,))],
        compiler_params=_params("arbitrary"),
        name="dispatch",
    )(dest_all, *h2ps)


def _combine_kernel(dcur_ref, dnext_ref, x1_ref, gw_ref, g2_ref, fg_ref, ys_ref, o_ref, buf, sems):
    tm = x1_ref.shape[0]
    rpt = buf.shape[2] // tm
    i = pl.program_id(0)
    n_steps = pl.num_programs(0)

    def issue(d_ref, s):
        def body(t, carry):
            for k in range(TOP_K):
                _rows_copy(ys_ref, d_ref[t * TOP_K + k] * rpt, buf.at[s, k], t * rpt, rpt,
                           sems.at[s]).start(priority=k % 2)
            return carry
        lax.fori_loop(0, tm, body, 0)

    def step(s):
        @pl.when(i == 0)
        def _():
            issue(dcur_ref, s)

        @pl.when(i + 1 < n_steps)
        def _():
            issue(dnext_ref, 1 - s)

        def drain(t, carry):
            for k in range(TOP_K):
                _rows_copy(ys_ref, 0, buf.at[s, k], 0, rpt, sems.at[s]).wait()
            return carry

        lax.fori_loop(0, tm, drain, 0)
        gw = gw_ref[...]
        y = None
        for k in range(TOP_K):
            yk = gw[:, k:k + 1] * _unpack_bf16_pairs(_load_tokens(buf.at[s, k], tm)).astype(F32)
            y = yk if y is None else y + yk
        x2 = x1_ref[...] + g2_ref[0] * y
        o_ref[...] = _rms(x2, fg_ref[...])

    _for_slot(i % 2, step)


def _combine(x1, gw, g2, final_g, ys, dest, seq):
    n, d = x1.shape
    tm = _tile(seq, ROW_TM)
    tps = seq // tm
    last = n // tm - 1
    return pl.pallas_call(
        _combine_kernel,
        grid=(n // tm,),
        in_specs=[pl.BlockSpec((tm * TOP_K,), lambda i: (i,), memory_space=pltpu.SMEM),
                  pl.BlockSpec((tm * TOP_K,), lambda i: (jnp.minimum(i + 1, last),), memory_space=pltpu.SMEM),
                  pl.BlockSpec((tm, d), lambda i: (i, 0)),
                  pl.BlockSpec((tm, LANES), lambda i: (i, 0)),
                  pl.BlockSpec((1, 1, d), lambda i: (i // tps, 0, 0)),
                  pl.BlockSpec((1, d), lambda i: (0, 0)),
                  pl.BlockSpec(memory_space=pl.ANY)],
        out_specs=pl.BlockSpec((tm, d), lambda i: (i, 0)),
        out_shape=jax.ShapeDtypeStruct((n, d), F32),
        scratch_shapes=[pltpu.VMEM((2, TOP_K, tm * (d // 2 // LANES), LANES), jnp.uint32),
                        pltpu.SemaphoreType.DMA((2,))],
        compiler_params=_params("arbitrary"),
        name="combine",
    )(dest, dest, x1, gw, g2, final_g.reshape(1, d), ys)


def _load_tokens(ref, n_tok):
    rpt = ref.shape[0] // n_tok
    return jnp.concatenate([ref[pl.ds(s, n_tok, stride=rpt), :] for s in range(rpt)], axis=1)


def _store_tokens(ref, x):
    n_tok = x.shape[0]
    rpt = x.shape[1] // LANES
    for s in range(rpt):
        ref[pl.ds(s, n_tok, stride=rpt), :] = x[:, s * LANES:(s + 1) * LANES]


def _moe_kernel(be_ref, bv_ref, x_ref, wg_ref, wu_ref, wd_ref, bg_ref, bu_ref, bd_ref, o_ref, xb_scr, acc):
    del be_ref
    valid = bv_ref[pl.program_id(0)] > 0
    f = pl.program_id(1)

    @pl.when(f == 0)
    def _():
        acc[...] = jnp.broadcast_to(bd_ref[0], acc.shape)

    @pl.when(jnp.logical_and(valid, f == 0))
    def _():
        xb_scr[...] = _unpack_bf16_pairs(_load_tokens(x_ref, xb_scr.shape[0]))

    @pl.when(valid)
    def _():
        xb = xb_scr[...]
        gate = jnp.dot(xb, wg_ref[0], preferred_element_type=F32) + bg_ref[0]
        up = jnp.dot(xb, wu_ref[0], preferred_element_type=F32) + bu_ref[0]
        gate = jnp.minimum(gate, SWIGLU_LIMIT)
        up = jnp.clip(up, -SWIGLU_LIMIT, SWIGLU_LIMIT)
        act = (up + 1.0) * (gate * jax.nn.sigmoid(SWIGLU_ALPHA * gate))
        acc[...] += jnp.dot(act.astype(BF16), wd_ref[0], preferred_element_type=F32)

    @pl.when(f == pl.num_programs(1) - 1)
    def _():
        _store_tokens(o_ref, _pack_bf16_pairs(acc[...].astype(BF16)))


def _moe(xs, block_e, block_valid, w_gu, b_gu, w_dn, b_dn):
    n_exp, d_ff, d = w_dn.shape
    rpt = d // 2 // LANES
    rows = xs.shape[0] // rpt
    fc = _tile(d_ff, 1024)
    nf = d_ff // fc
    n_blocks = rows // MOE_BM
    grid_spec = pltpu.PrefetchScalarGridSpec(
        num_scalar_prefetch=2,
        grid=(n_blocks, nf),
        in_specs=[pl.BlockSpec((MOE_BM * rpt, LANES), lambda i, f, be, bv: (i, 0)),
                  pl.BlockSpec((1, d, fc), lambda i, f, be, bv: (be[i], 0, f)),
                  pl.BlockSpec((1, d, fc), lambda i, f, be, bv: (be[i], 0, nf + f)),
                  pl.BlockSpec((1, fc, d), lambda i, f, be, bv: (be[i], f, 0)),
                  pl.BlockSpec((1, 1, fc), lambda i, f, be, bv: (be[i], 0, f)),
                  pl.BlockSpec((1, 1, fc), lambda i, f, be, bv: (be[i], 0, nf + f)),
                  pl.BlockSpec((1, 1, d), lambda i, f, be, bv: (be[i], 0, 0))],
        out_specs=pl.BlockSpec((MOE_BM * rpt, LANES), lambda i, f, be, bv: (i, 0)),
        scratch_shapes=[pltpu.VMEM((MOE_BM, d), BF16), pltpu.VMEM((MOE_BM, d), F32)],
    )
    return pl.pallas_call(
        _moe_kernel,
        grid_spec=grid_spec,
        out_shape=jax.ShapeDtypeStruct((rows * rpt, LANES), jnp.uint32),
        compiler_params=_params("arbitrary", "arbitrary"),
        name="moe",
    )(block_e, block_valid, xs, w_gu, w_gu, w_dn,
      b_gu.reshape(n_exp, 1, 2 * d_ff), b_gu.reshape(n_exp, 1, 2 * d_ff), b_dn.reshape(n_exp, 1, d))


def _mixer_and_router(x, ada, wts, base0):
    (ln1_g, ln2_g, w_main, w_gates, b_gates, conv_w, conv_b, head_norm_g, w_four, w_out_b, mc,
     wr_hi, wr_lo, b_router) = wts
    bsz, seq, d = x.shape
    n = bsz * seq
    mw = d // 2
    dh = mw // N_HEADS
    x2 = x.reshape(n, d)
    sh1, sc1, g1, sh2, sc2, g2 = [a.reshape(bsz, 1, d) for a in jnp.split(ada, 6, axis=-1)]

    proj, gates = _inproj(x2, ln1_g, sc1, sh1, w_main, w_gates, b_gates, seq)
    q, kt = _conv(proj, conv_w, conv_b, seq, mw, dh ** -0.5)
    gates_t = gates[:, :N_GATES].reshape(bsz, seq, N_GATES).transpose(0, 2, 1).reshape(bsz * N_GATES, seq)
    hf, hb = _mlstm(q, kt, proj, gates, gates_t, bsz, seq, mw)
    assert d - mw == mw, "the Fourier block is addressed as column block 4 of proj"
    pdft = _position_dft(proj, bsz, seq, mw, 4)
    x1, h2p, idx, gw, rank, cnt = _mix(hf, hb, proj, pdft, x2, g1, sc2, sh2, head_norm_g, ln2_g,
                                       w_out_b, mc, w_four, wr_hi, wr_lo, b_router, base0, seq)
    return dict(x1=x1, h2p=h2p, idx=idx[:, :TOP_K], gw=gw, rank=rank[:, :TOP_K].astype(jnp.int32),
                cnt=cnt, g2=g2, shape=(bsz, seq, d))


def kernel(x_prompt, x_sample, c_prompt, c_sample, ln1_g, ln2_g, w_ada, b_ada, w_in, b_gates, conv_w,
           conv_b, head_norm_g, w_four, w_out, w_router, b_router, w_gate_up, b_gate_up, w_down, b_down,
           final_g):
    assert w_ada.shape[0] == 1, "single-layer trunk"
    d = x_prompt.shape[-1]
    mw = d // 2
    cg = (d - mw) // N_GROUPS
    w_in0 = w_in[0]
    w_main = jnp.concatenate([w_in0[:, :4 * mw], w_in0[:, 4 * mw + N_GATES:]], axis=1).astype(BF16)
    w_gates = jnp.pad(w_in0[:, 4 * mw:4 * mw + N_GATES], ((0, 0), (0, LANES - N_GATES))).astype(BF16)
    bg = jnp.pad(b_gates[0], (0, LANES - N_GATES)).reshape(1, LANES)
    kc = np.arange(cg)
    angc = 2.0 * np.pi * np.outer(kc, kc) / cg
    mc = jnp.asarray(np.concatenate([np.cos(angc), -np.sin(angc)], axis=0) / math.sqrt(cg), BF16)
    wr = w_router[0]
    wr_hi = wr.astype(BF16)
    wr_lo = (wr - wr_hi.astype(F32)).astype(BF16)
    wts = (ln1_g[0], ln2_g[0], w_main, w_gates, bg, conv_w[0], conv_b[0], head_norm_g[0], w_four[0],
           w_out[0].astype(BF16), mc, wr_hi, wr_lo, b_router[0])
    n_exp = wr.shape[1]
    nbp = c_prompt.shape[0]
    ada = _ada(jnp.concatenate([c_prompt, c_sample], axis=0), w_ada[0], b_ada[0])
    gp = _mixer_and_router(x_prompt, ada[:nbp], wts, jnp.zeros((1, n_exp), F32))
    gs = _mixer_and_router(x_sample, ada[nbp:], wts, gp["cnt"])
    groups = (gp, gs)

    counts = gs["cnt"][0].astype(jnp.int32)
    p_counts = (counts + MOE_BM - 1) // MOE_BM * MOE_BM
    p_ends = jnp.cumsum(p_counts)
    p_off = p_ends - p_counts
    n_tok = sum(g["x1"].shape[0] for g in groups)
    n_blocks = (n_tok * TOP_K) // MOE_BM + n_exp
    starts = jnp.arange(n_blocks, dtype=jnp.int32) * MOE_BM
    block_e = jnp.minimum(jnp.sum((starts[:, None] >= p_ends[None, :]).astype(jnp.int32), axis=1), n_exp - 1)
    block_valid = (starts < p_ends[-1]).astype(jnp.int32)
    dests = [(p_off[g["idx"]] + g["rank"]).reshape(-1) for g in groups]
    rows = n_blocks * MOE_BM
    gap_start = jnp.concatenate([p_off + counts, p_ends[-1:]])
    gap_len = jnp.concatenate([p_counts - counts, rows - p_ends[-1:]])
    gap_end = jnp.cumsum(gap_len)
    j = jnp.arange(n_exp * MOE_BM, dtype=jnp.int32)
    gi = jnp.sum((j[:, None] >= gap_end[None, :]).astype(jnp.int32), axis=1)
    pad_dest = (gap_start[gi] + (j - (gap_end - gap_len)[gi])).astype(jnp.int32)

    xs = _dispatch([g["h2p"] for g in groups], jnp.concatenate(dests + [pad_dest]), d // 2 // LANES)
    ys = _moe(xs, block_e, block_valid, w_gate_up[0].astype(BF16), b_gate_up[0], w_down[0].astype(BF16), b_down[0])
    outs = []
    for g, dest in zip(groups, dests):
        bsz, seq, _ = g["shape"]
        outs.append(_combine(g["x1"], g["gw"], g["g2"], final_g, ys, dest, seq).reshape(g["shape"]))
    return tuple(outs)
```

```python
import functools
import math

import numpy as np
import jax
import jax.numpy as jnp
from jax import lax
from jax.experimental import pallas as pl
from jax.experimental.pallas import tpu as pltpu

F32 = jnp.float32
BF16 = jnp.bfloat16

N_HEADS = 4
N_GROUPS = 4
N_GATES = 4 * N_HEADS
CHUNK = 128
TOP_K = 4
SWIGLU_LIMIT = 7.0
SWIGLU_ALPHA = 1.702
EPS = 1e-6
LANES = 128
SUBLANES = 8
DFT_N2 = 128
MOE_BM = 512
ROW_TM = 256
VMEM_LIMIT = 56 * 1024 * 1024


def _params(*sem, flags=None):
    return pltpu.CompilerParams(dimension_semantics=sem, vmem_limit_bytes=VMEM_LIMIT, flags=flags)


def _tile(n, pref):
    t = min(n, pref)
    while n % t:
        t //= 2
    return t


def _ada_kernel(c_ref, w_ref, b_ref, o_ref):
    c = c_ref[...]
    s = (c * jax.nn.sigmoid(c)).astype(BF16)
    o_ref[...] = jnp.dot(s, w_ref[...].astype(BF16), preferred_element_type=F32) + b_ref[...]


def _ada(c, w_ada, b_ada):
    bsz, d = c.shape
    bp = -(-bsz // SUBLANES) * SUBLANES
    cp = jnp.pad(c, ((0, bp - bsz), (0, 0)))
    n_out = w_ada.shape[1]
    tn = _tile(n_out, 1024)
    out = pl.pallas_call(
        _ada_kernel,
        grid=(n_out // tn,),
        in_specs=[pl.BlockSpec((bp, d), lambda j: (0, 0)),
                  pl.BlockSpec((d, tn), lambda j: (0, j)),
                  pl.BlockSpec((1, tn), lambda j: (0, j))],
        out_specs=pl.BlockSpec((bp, tn), lambda j: (0, j)),
        out_shape=jax.ShapeDtypeStruct((bp, n_out), F32),
        compiler_params=_params("arbitrary"),
        name="ada",
    )(cp, w_ada, b_ada.reshape(1, n_out))
    return out[:bsz]


def _rms(x, g):
    return (x * lax.rsqrt(jnp.mean(x * x, axis=-1, keepdims=True) + EPS)) * g


def _inproj_kernel(x_ref, g_ref, sc_ref, sh_ref, w_ref, wg_ref, bg_ref, o_ref, og_ref, h_scr):
    @pl.when(pl.program_id(1) == 0)
    def _():
        h = _rms(x_ref[...], g_ref[...]) * (1.0 + sc_ref[0]) + sh_ref[0]
        hb = h.astype(BF16)
        h_scr[...] = hb
        og_ref[...] = jnp.dot(hb, wg_ref[...], preferred_element_type=F32) + bg_ref[...]

    o_ref[...] = jnp.dot(h_scr[...], w_ref[...], preferred_element_type=F32)


def _inproj(x2, ln_g, sc, sh, w_main, w_gates, b_gates, seq):
    n, d = x2.shape
    p = w_main.shape[1]
    tm = _tile(seq, 1024)
    tn = _tile(p, 1024)
    tps = seq // tm
    return pl.pallas_call(
        _inproj_kernel,
        grid=(n // tm, p // tn),
        in_specs=[pl.BlockSpec((tm, d), lambda i, j: (i, 0)),
                  pl.BlockSpec((1, d), lambda i, j: (0, 0)),
                  pl.BlockSpec((1, 1, d), lambda i, j: (i // tps, 0, 0)),
                  pl.BlockSpec((1, 1, d), lambda i, j: (i // tps, 0, 0)),
                  pl.BlockSpec((d, tn), lambda i, j: (0, j)),
                  pl.BlockSpec((d, LANES), lambda i, j: (0, 0)),
                  pl.BlockSpec((1, LANES), lambda i, j: (0, 0))],
        out_specs=[pl.BlockSpec((tm, tn), lambda i, j: (i, j)),
                   pl.BlockSpec((tm, LANES), lambda i, j: (i, 0))],
        out_shape=[jax.ShapeDtypeStruct((n, p), F32),
                   jax.ShapeDtypeStruct((n, LANES), F32)],
        scratch_shapes=[pltpu.VMEM((tm, d), BF16)],
        compiler_params=_params("arbitrary", "arbitrary"),
        name="inproj",
    )(x2, ln_g.reshape(1, d), sc, sh, w_main, w_gates, b_gates)


def _conv_kernel(x_ref, prev_ref, next_ref, w_ref, b_ref, q_ref, kt_ref, res_scr, *, tps, k_scale):
    it = pl.program_id(0) % tps
    w = w_ref[...]
    b = b_ref[...]
    taps = w.shape[0]
    half = taps // 2

    def conv_act(z):
        nz = z.shape[0]
        acc = z * w[half:half + 1]
        for j in range(taps):
            if j != half:
                acc = acc + pltpu.roll(z, (half - j) % nz, 0) * w[j:j + 1]
        acc = acc + b
        return acc * jax.nn.sigmoid(acc)

    x = x_ref[...]
    tm = x.shape[0]
    res_scr[...] = conv_act(x)
    prev = jnp.where(it == 0, 0.0, prev_ref[...])
    nxt = jnp.where(it == tps - 1, 0.0, next_ref[...])
    top = conv_act(jnp.concatenate([prev, x[0:2 * SUBLANES]], axis=0))
    res_scr[0:SUBLANES, :] = top[SUBLANES:2 * SUBLANES]
    bot = conv_act(jnp.concatenate([x[tm - 2 * SUBLANES:tm], nxt], axis=0))
    res_scr[tm - SUBLANES:tm, :] = bot[SUBLANES:2 * SUBLANES]

    @pl.when(pl.program_id(1) == 0)
    def _():
        q_ref[...] = res_scr[...]

    @pl.when(pl.program_id(1) == 1)
    def _():
        for r in range(0, tm, LANES):
            kt_ref[:, r:r + LANES] = (res_scr[r:r + LANES, :] * k_scale).T


def _conv(proj, conv_w, conv_b, seq, mw, k_scale):
    n = proj.shape[0]
    taps = conv_w.shape[0]
    assert taps // 2 <= SUBLANES
    tm = _tile(seq, 512)
    assert tm >= 4 * SUBLANES
    tps = seq // tm
    r8 = tm // SUBLANES
    last8 = n // SUBLANES - 1
    return pl.pallas_call(
        functools.partial(_conv_kernel, tps=tps, k_scale=k_scale),
        grid=(n // tm, 2),
        in_specs=[pl.BlockSpec((tm, mw), lambda i, j: (i, j)),
                  pl.BlockSpec((SUBLANES, mw), lambda i, j: (jnp.maximum(i * r8 - 1, 0), j)),
                  pl.BlockSpec((SUBLANES, mw), lambda i, j: (jnp.minimum((i + 1) * r8, last8), j)),
                  pl.BlockSpec((taps, mw), lambda i, j: (0, j)),
                  pl.BlockSpec((1, mw), lambda i, j: (0, j))],
        out_specs=[pl.BlockSpec((tm, mw), lambda i, j: (i, 0)),
                   pl.BlockSpec((mw, tm), lambda i, j: (0, i))],
        out_shape=[jax.ShapeDtypeStruct((n, mw), F32), jax.ShapeDtypeStruct((mw, n), F32)],
        scratch_shapes=[pltpu.VMEM((tm, mw), F32)],
        compiler_params=_params("arbitrary", "arbitrary"),
        name="conv",
    )(proj, proj, proj, conv_w, conv_b.reshape(1, 2 * mw))


def _log_sigmoid(x):
    return jnp.minimum(x, 0.0) - jnp.log1p(jnp.exp(-jnp.abs(x)))


def _split3(x):
    hi = x.astype(BF16)
    r = x - hi.astype(F32)
    mid = r.astype(BF16)
    lo = (r - mid.astype(F32)).astype(BF16)
    return hi, mid, lo


def _cummax(x, axis, reverse):
    n = x.shape[axis]
    pos = lax.broadcasted_iota(jnp.int32, x.shape, axis)
    step = 1
    while step < n:
        if reverse:
            shifted = jnp.where(pos < n - step, pltpu.roll(x, n - step, axis), -jnp.inf)
        else:
            shifted = jnp.where(pos >= step, pltpu.roll(x, step, axis), -jnp.inf)
        x = jnp.maximum(x, shifted)
        step *= 2
    return x


def _mlstm_gates(gc, gr, tri_c, tri_r, m_lane, m_sub, reverse):
    length = gc.shape[0]
    last = 0 if reverse else length - 1
    b_cols = jnp.dot(tri_c, jnp.concatenate(_split3(_log_sigmoid(gc)), axis=0), preferred_element_type=F32)
    b_rows = jnp.dot(jnp.concatenate(_split3(_log_sigmoid(gr)), axis=1), tri_r, preferred_element_type=F32)
    b_cols = pltpu.roll(b_cols, LANES - N_HEADS, 1)
    b_rows = pltpu.roll(b_rows, N_GATES - N_HEADS, 0)
    u_cols = gc - b_cols
    u_rows = gr - b_rows
    c_cols = jnp.maximum(m_lane, _cummax(u_cols, 0, reverse))
    c_rows = jnp.maximum(m_sub, _cummax(u_rows, 1, reverse))
    c_last = c_rows[:, last:last + 1]
    out = dict(
        c_cols=c_cols, u_rows=u_rows,
        ea_cols=jnp.exp(m_lane - c_cols),
        floor_cols=jnp.exp(-b_cols - c_cols),
        ew_rows=jnp.exp(u_rows - c_last),
        decay_rows=jnp.exp(m_sub - c_last),
        m_lane=b_cols[last:last + 1, :] + c_cols[last:last + 1, :],
        m_sub=jnp.broadcast_to(b_rows[:, last:last + 1] + c_last, m_sub.shape),
    )
    return out


def _mlstm_heads(work, gate_fn, cn_ref):
    length = work[0][0].shape[0]
    row = lax.broadcasted_iota(jnp.int32, (length, length), 0)
    col = lax.broadcasted_iota(jnp.int32, (length, length), 1)
    qxs = []
    for q_ref, kt_ref, v_ref, h_ref, sl, di, ci, idx, reverse in work:
        rhs = jnp.concatenate([kt_ref[sl, :].astype(BF16), cn_ref[idx].astype(BF16)], axis=1)
        qxs.append(jnp.dot(q_ref[:, sl].astype(BF16), rhs, preferred_element_type=F32))
    gts = gate_fn()
    work = [w[:5] + (gts[w[5]],) + w[6:] for w in work]
    wms, lhs, wvs = [], [], []
    for (q_ref, kt_ref, v_ref, h_ref, sl, gt, ci, idx, reverse), qx in zip(work, qxs):
        mask = (col >= row) if reverse else (col <= row)
        wm = jnp.exp(jnp.where(mask, gt["u_rows"][ci:ci + 1, :] - gt["c_cols"][:, ci:ci + 1], -jnp.inf))
        wm = wm * qx[:, :length]
        kw = kt_ref[sl, :] * gt["ew_rows"][ci:ci + 1, :]
        wms.append(wm)
        lhs.append(jnp.concatenate([wm, kw], axis=0).astype(BF16))
    for (q_ref, kt_ref, v_ref, h_ref, sl, gt, ci, idx, reverse), lh in zip(work, lhs):
        v1 = jnp.concatenate([v_ref[:, sl].astype(BF16), jnp.ones((length, LANES), BF16)], axis=1)
        wvs.append(jnp.dot(lh, v1, preferred_element_type=F32))
    for (q_ref, kt_ref, v_ref, h_ref, sl, gt, ci, idx, reverse), qx, wm, wv in zip(work, qxs, wms, wvs):
        dh = wv.shape[1] - LANES
        ea = gt["ea_cols"][:, ci:ci + 1]
        num = wv[:length, :dh] + ea * qx[:, length:length + dh]
        den = jnp.sum(wm, axis=1, keepdims=True) + ea * qx[:, length + dh:length + dh + 1]
        h_ref[:, sl] = num / jnp.maximum(jnp.abs(den), gt["floor_cols"][:, ci:ci + 1])
        width = wv.shape[1]
        decay = jnp.concatenate([gt["decay_rows"][ci:ci + 1, :]] * pl.cdiv(width, LANES), axis=1)[:, :width]
        cn_ref[idx] = decay * cn_ref[idx] + wv[length:, :]


def _mlstm_kernel(qf_ref, kf_ref, vf_ref, gcf_ref, grf_ref,
                  qb_ref, kb_ref, vb_ref, gcb_ref, grb_ref,
                  l3_ref, u3_ref, l3t_ref, u3t_ref,
                  hf_ref, hb_ref, cn_scr, ml_scr, ms_scr):
    @pl.when(pl.program_id(1) == 0)
    def _():
        cn_scr[...] = jnp.zeros_like(cn_scr)
        ml_scr[...] = jnp.zeros_like(ml_scr)
        ms_scr[...] = jnp.zeros_like(ms_scr)

    dh = qf_ref.shape[1] // N_HEADS
    dirs = (
        (False, qf_ref, kf_ref, vf_ref, gcf_ref, grf_ref, l3_ref, u3t_ref, hf_ref),
        (True, qb_ref, kb_ref, vb_ref, gcb_ref, grb_ref, u3_ref, l3t_ref, hb_ref),
    )
    work = []
    for di, (reverse, q_ref, kt_ref, v_ref, gc_ref, gr_ref, tri_c, tri_r, h_ref) in enumerate(dirs):
        for hd in range(N_HEADS):
            sl = slice(hd * dh, (hd + 1) * dh)
            work.append((q_ref, kt_ref, v_ref, h_ref, sl, di, di * 2 * N_HEADS + hd, di * N_HEADS + hd, reverse))

    def gate_fn():
        gts = []
        for di, (reverse, q_ref, kt_ref, v_ref, gc_ref, gr_ref, tri_c, tri_r, h_ref) in enumerate(dirs):
            gt = _mlstm_gates(gc_ref[...], gr_ref[...], tri_c[...], tri_r[...], ml_scr[di], ms_scr[di], reverse)
            ml_scr[di] = gt["m_lane"]
            ms_scr[di] = gt["m_sub"]
            gts.append(gt)
        return gts

    _mlstm_heads(work, gate_fn, cn_scr)


def _tri_consts(length):
    lower = np.tril(np.ones((length, length), np.float32))
    upper = lower.T
    l3 = np.concatenate([lower] * 3, axis=1)
    u3 = np.concatenate([upper] * 3, axis=1)
    l3t = np.concatenate([lower] * 3, axis=0)
    u3t = np.concatenate([upper] * 3, axis=0)
    return tuple(jnp.asarray(a, BF16) for a in (l3, u3, l3t, u3t))


def _mlstm(q, kt, proj, gates, gates_t, bsz, seq, mw):
    n = q.shape[0]
    nc = seq // CHUNK
    dh = mw // N_HEADS
    assert CHUNK == LANES
    fwd = lambda b, c: b * nc + c
    bwd = lambda b, c: b * nc + (nc - 1 - c)
    cst = lambda b, c: (0, 0)
    in_specs = []
    for pos in (fwd, bwd):
        in_specs += [
            pl.BlockSpec((CHUNK, mw), lambda b, c, pos=pos: (pos(b, c), 0)),
            pl.BlockSpec((mw, CHUNK), lambda b, c, pos=pos: (0, pos(b, c))),
            pl.BlockSpec((CHUNK, mw), lambda b, c, pos=pos: (pos(b, c), 2)),
            pl.BlockSpec((CHUNK, LANES), lambda b, c, pos=pos: (pos(b, c), 0)),
            pl.BlockSpec((N_GATES, CHUNK), lambda b, c, pos=pos: (b, pos(0, c))),
        ]
    in_specs += [pl.BlockSpec((CHUNK, 3 * CHUNK), cst), pl.BlockSpec((CHUNK, 3 * CHUNK), cst),
                 pl.BlockSpec((3 * CHUNK, CHUNK), cst), pl.BlockSpec((3 * CHUNK, CHUNK), cst)]
    return pl.pallas_call(
        _mlstm_kernel,
        grid=(bsz, nc),
        in_specs=in_specs,
        out_specs=[pl.BlockSpec((CHUNK, mw), lambda b, c: (fwd(b, c), 0)),
                   pl.BlockSpec((CHUNK, mw), lambda b, c: (bwd(b, c), 0))],
        out_shape=[jax.ShapeDtypeStruct((n, mw), F32), jax.ShapeDtypeStruct((n, mw), F32)],
        scratch_shapes=[pltpu.VMEM((2 * N_HEADS, dh, dh + LANES), F32),
                        pltpu.VMEM((2, 1, LANES), F32),
                        pltpu.VMEM((2, N_GATES, LANES), F32)],
        compiler_params=_params("arbitrary", "arbitrary"),
        name="mlstm",
    )(q, kt, proj, gates, gates_t, q, kt, proj, gates, gates_t, *_tri_consts(CHUNK))


def _dft_a_kernel(x_ref, ga_ref, tc_ref, ts_ref, o_ref, *, jb):
    n1, width = x_ref.shape[1], x_ref.shape[3]
    rows = n1 * SUBLANES
    reps = width // LANES
    ga = ga_ref[...]
    for s in range(jb):
        sl = slice(s * SUBLANES, (s + 1) * SUBLANES)
        x = x_ref[0, :, sl, :].reshape(rows, width).astype(BF16)
        z = jnp.dot(ga, x, preferred_element_type=F32)
        zc, zs = z[:rows], z[rows:]
        tc = jnp.concatenate([tc_ref[s]] * reps, axis=1)
        ts = jnp.concatenate([ts_ref[s]] * reps, axis=1)
        o_ref[0, 0, :, sl, :] = (zc * tc - zs * ts).reshape(n1, SUBLANES, width)
        o_ref[0, 1, :, sl, :] = (zc * ts + zs * tc).reshape(n1, SUBLANES, width)


def _dft_b_kernel(z_ref, gb_ref, o_ref, scr):
    n2, wb = z_ref.shape[3], z_ref.shape[4]
    gb = gb_ref[...]
    for k in range(SUBLANES):
        slab = z_ref[0, :, k].reshape(2 * n2, wb).astype(BF16)
        p = jnp.dot(gb, slab, preferred_element_type=F32)
        for cs in range(2):
            for c in range(wb // LANES):
                scr[c, pl.ds(cs * n2 * SUBLANES + k, n2, stride=SUBLANES), :] = (
                    p[cs * n2:(cs + 1) * n2, c * LANES:(c + 1) * LANES])
    for c in range(wb // LANES):
        o_ref[0, :, :, :, c * LANES:(c + 1) * LANES] = scr[c].reshape(2, n2, SUBLANES, LANES)


def _position_dft(proj, bsz, seq, width, col_block):
    n2 = DFT_N2
    n1 = seq // n2
    assert seq % n2 == 0 and n1 % SUBLANES == 0 and proj.shape[1] % width == 0
    rows = n1 * SUBLANES
    k1 = np.arange(n1)
    ang1 = 2.0 * np.pi * np.outer(k1, k1) / n1
    eye = np.eye(SUBLANES)
    ga = jnp.asarray(np.concatenate([np.kron(np.cos(ang1), eye), np.kron(np.sin(ang1), eye)], axis=0)
                     / math.sqrt(n1), BF16)
    k2 = np.arange(n2)
    ang2 = 2.0 * np.pi * np.outer(k2, k2) / n2
    c2, s2 = np.cos(ang2), np.sin(ang2)
    gb = jnp.asarray(np.block([[c2, -s2], [s2, c2]]) / math.sqrt(n2), BF16)
    n2_idx = jnp.arange(n2, dtype=jnp.int32).reshape(n2 // SUBLANES, 1, SUBLANES)
    k1_idx = jnp.arange(n1, dtype=jnp.int32).reshape(1, n1, 1)
    ang = ((2.0 * math.pi / seq) * ((n2_idx * k1_idx) % seq).astype(F32)).reshape(n2 // SUBLANES, rows)
    tc = jnp.broadcast_to(jnp.cos(ang)[:, :, None], (n2 // SUBLANES, rows, LANES))
    ts = jnp.broadcast_to(jnp.sin(ang)[:, :, None], (n2 // SUBLANES, rows, LANES))

    jb = max(1, min(n2 // SUBLANES, 512 // rows))
    proj4 = proj.reshape(bsz, n1, n2, proj.shape[1])
    z = pl.pallas_call(
        functools.partial(_dft_a_kernel, jb=jb),
        grid=(bsz, n2 // (SUBLANES * jb)),
        in_specs=[pl.BlockSpec((1, n1, SUBLANES * jb, width), lambda b, j: (b, 0, j, col_block)),
                  pl.BlockSpec((2 * rows, rows), lambda b, j: (0, 0)),
                  pl.BlockSpec((jb, rows, LANES), lambda b, j: (j, 0, 0)),
                  pl.BlockSpec((jb, rows, LANES), lambda b, j: (j, 0, 0))],
        out_specs=pl.BlockSpec((1, 2, n1, SUBLANES * jb, width), lambda b, j: (b, 0, 0, j, 0)),
        out_shape=jax.ShapeDtypeStruct((bsz, 2, n1, n2, width), F32),
        compiler_params=_params("arbitrary", "arbitrary"),
        name="dft_a",
    )(proj4, ga, tc, ts)
    wb = _tile(width, 512)
    p = pl.pallas_call(
        _dft_b_kernel,
        grid=(bsz, n1 // SUBLANES, width // wb),
        in_specs=[pl.BlockSpec((1, 2, SUBLANES, n2, wb), lambda b, j, w: (b, 0, j, 0, w)),
                  pl.BlockSpec((2 * n2, 2 * n2), lambda b, j, w: (0, 0))],
        out_specs=pl.BlockSpec((1, 2, n2, SUBLANES, wb), lambda b, j, w: (b, 0, 0, j, w)),
        out_shape=jax.ShapeDtypeStruct((bsz, 2, n2, n1, width), F32),
        scratch_shapes=[pltpu.VMEM((wb // LANES, 2 * n2 * SUBLANES, LANES), F32)],
        compiler_params=_params("arbitrary", "arbitrary", "arbitrary"),
        name="dft_b",
    )(z, gb)
    return p.reshape(bsz, 2, seq, width)


def _pack_bf16_pairs(xb):
    c = xb.shape[1] // 2
    lo = pltpu.bitcast(xb[:, :c].astype(F32), jnp.uint32)
    hi = pltpu.bitcast(xb[:, c:].astype(F32), jnp.uint32)
    return (lo >> 16) | hi


def _unpack_bf16_pairs(w):
    lo = pltpu.bitcast(w << 16, F32).astype(BF16)
    hi = pltpu.bitcast(w & jnp.uint32(0xFFFF0000), F32).astype(BF16)
    return jnp.concatenate([lo, hi], axis=1)


def _mix_kernel(hf_ref, hb_ref, op_ref, pc_ref, ps_ref, x_ref, g1_ref, sc_ref, sh_ref,
                hng_ref, ln_ref, wout_ref, mc_ref, wf_ref, wrh_ref, br_ref, ltri_ref, base0_ref,
                x1_ref, h2_ref, idx_ref, gw_ref, rank_ref, cnt_ref, base_scr):
    @pl.when(pl.program_id(0) == 0)
    def _():
        base_scr[...] = base0_ref[...]

    tm, mw = hf_ref.shape
    dh = mw // N_HEADS
    hm = hf_ref[...] + hb_ref[...]
    parts = []
    for hd in range(N_HEADS):
        seg = hm[:, hd * dh:(hd + 1) * dh]
        parts.append(seg * lax.rsqrt(jnp.mean(seg * seg, axis=-1, keepdims=True) + EPS))
    hm = jnp.concatenate(parts, axis=1) * hng_ref[...] * jax.nn.sigmoid(op_ref[...])
    cg = pc_ref.shape[3] // N_GROUPS
    pc = pc_ref[0, 0]
    ps = ps_ref[0, 0]
    for gi in range(N_GROUPS):
        sl = slice(gi * cg, (gi + 1) * cg)
        spec = jnp.dot(jnp.concatenate([pc[:, sl], ps[:, sl]], axis=1).astype(BF16), mc_ref[...],
                       preferred_element_type=F32)
        parts.append(jnp.dot(spec.astype(BF16), wf_ref[gi].astype(BF16), preferred_element_type=F32))
    cat = jnp.concatenate([hm] + parts[N_HEADS:], axis=1).astype(BF16)
    mix = jnp.dot(cat, wout_ref[...], preferred_element_type=F32)
    x1 = x_ref[...] + g1_ref[0] * mix
    x1_ref[...] = x1
    h2 = _rms(x1, ln_ref[...]) * (1.0 + sc_ref[0]) + sh_ref[0]
    hh = h2.astype(BF16)
    _store_tokens(h2_ref, _pack_bf16_pairs(hh))

    n_exp = br_ref.shape[1]
    hl = (h2 - hh.astype(F32)).astype(BF16)
    rl = jnp.dot(jnp.concatenate([hh, hl], axis=0), wrh_ref[...], preferred_element_type=F32)
    logits = (rl[:tm, :n_exp] + rl[tm:, :n_exp] + rl[:tm, n_exp:2 * n_exp]) + br_ref[...]
    lane = lax.broadcasted_iota(jnp.int32, (tm, n_exp), 1).astype(F32)
    lane_o = lax.broadcasted_iota(jnp.int32, (tm, LANES), 1)
    base = base_scr[...]
    ltri = ltri_ref[...]
    idx_out = jnp.zeros((tm, LANES), F32)
    val_out = jnp.zeros((tm, LANES), F32)
    rank_out = jnp.zeros((tm, LANES), F32)
    top0 = None
    denom = jnp.zeros((tm, 1), F32)
    work = logits
    for k in range(TOP_K):
        mx = jnp.max(work, axis=1, keepdims=True)
        ik = jnp.min(jnp.where(work == mx, lane, float(n_exp)), axis=1, keepdims=True)
        hit = lane == ik
        work = jnp.where(hit, -jnp.inf, work)
        if top0 is None:
            top0 = mx
        ek = jnp.exp(mx - top0)
        denom = denom + ek
        onehot = hit.astype(F32)
        before = jnp.dot(ltri, onehot.astype(BF16), preferred_element_type=F32)
        rk = jnp.sum(jnp.where(hit, base + before, 0.0), axis=1, keepdims=True)
        base = base + jnp.sum(onehot, axis=0, keepdims=True)
        idx_out = jnp.where(lane_o == k, ik, idx_out)
        val_out = jnp.where(lane_o == k, ek, val_out)
        rank_out = jnp.where(lane_o == k, rk, rank_out)
    idx_ref[...] = idx_out.astype(jnp.int32)
    gw_ref[...] = val_out / denom
    rank_ref[...] = rank_out
    base_scr[...] = base
    cnt_ref[...] = base


def _mix(hf, hb, proj, pdft, x2, g1, sc2, sh2, hng, ln2_g, w_out, mc, w_four, wr_cat, b_router,
         base0, seq):
    n, d = x2.shape
    mw = hf.shape[1]
    fw = pdft.shape[3]
    n_exp = b_router.shape[0]
    cg = fw // N_GROUPS
    tm = _tile(seq, 256)
    tps = seq // tm
    ltri = jnp.asarray(np.tril(np.ones((tm, tm), np.float32), -1), BF16)
    row = lambda i: (i, 0)
    cst = lambda i: (0, 0)
    bat = lambda i: (i // tps, 0, 0)
    return pl.pallas_call(
        _mix_kernel,
        grid=(n // tm,),
        in_specs=[pl.BlockSpec((tm, mw), row), pl.BlockSpec((tm, mw), row),
                  pl.BlockSpec((tm, mw), lambda i: (i, 3)),
                  pl.BlockSpec((1, 1, tm, fw), lambda i: (i // tps, 0, i % tps, 0)),
                  pl.BlockSpec((1, 1, tm, fw), lambda i: (i // tps, 1, i % tps, 0)),
                  pl.BlockSpec((tm, d), row),
                  pl.BlockSpec((1, 1, d), bat), pl.BlockSpec((1, 1, d), bat), pl.BlockSpec((1, 1, d), bat),
                  pl.BlockSpec((1, mw), cst), pl.BlockSpec((1, d), cst),
                  pl.BlockSpec((mw + fw, d), cst),
                  pl.BlockSpec((2 * cg, cg), cst),
                  pl.BlockSpec((N_GROUPS, cg, cg), lambda i: (0, 0, 0)),
                  pl.BlockSpec((d, wr_cat.shape[1]), cst),
                  pl.BlockSpec((1, n_exp), cst),
                  pl.BlockSpec((tm, tm), cst),
                  pl.BlockSpec((1, n_exp), cst)],
        out_specs=[pl.BlockSpec((tm, d), row), pl.BlockSpec((tm * (d // 2 // LANES), LANES), row),
                   pl.BlockSpec((tm, LANES), row), pl.BlockSpec((tm, LANES), row),
                   pl.BlockSpec((tm, LANES), row), pl.BlockSpec((1, n_exp), cst)],
        out_shape=[jax.ShapeDtypeStruct((n, d), F32), jax.ShapeDtypeStruct((n * (d // 2 // LANES), LANES), jnp.uint32),
                   jax.ShapeDtypeStruct((n, LANES), jnp.int32), jax.ShapeDtypeStruct((n, LANES), F32),
                   jax.ShapeDtypeStruct((n, LANES), F32), jax.ShapeDtypeStruct((1, n_exp), F32)],
        scratch_shapes=[pltpu.VMEM((1, n_exp), F32)],
        compiler_params=_params("arbitrary"),
        name="mix",
    )(hf, hb, proj, pdft, pdft, x2, g1, sc2, sh2, hng.reshape(1, mw), ln2_g.reshape(1, d),
      w_out, mc, w_four, wr_cat, b_router.reshape(1, n_exp), ltri, base0)


def _rows_copy(src, src_row, dst, dst_row, nrows, sem):
    if nrows > 1:
        src_row = pl.multiple_of(src_row, nrows)
        dst_row = pl.multiple_of(dst_row, nrows)
    return pltpu.make_async_copy(src.at[pl.ds(src_row, nrows), :], dst.at[pl.ds(dst_row, nrows), :], sem)


def _for_slot(slot, body):
    for s in (0, 1):
        pl.when(slot == s)(functools.partial(body, s))


def _dispatch_kernel(dest_ref, *refs, tiles, rpt):
    h_refs, (xs_ref, hbuf, sems) = refs[:len(tiles)], refs[len(tiles):]
    tm = hbuf.shape[1] // rpt
    i = pl.program_id(0)
    n_steps = pl.num_programs(0)

    def drain(s):
        def body(t, carry):
            for _ in range(TOP_K):
                _rows_copy(hbuf.at[s], 0, xs_ref, 0, rpt, sems.at[s]).wait()
            return carry
        lax.fori_loop(0, tm, body, 0)

    def step(s):
        @pl.when(i >= 2)
        def _():
            drain(s)

        first = 0
        for h_ref, cnt in zip(h_refs, tiles):
            @pl.when(jnp.logical_and(i >= first, i < first + cnt))
            def _(h_ref=h_ref):
                hbuf[s] = h_ref[...]
            first += cnt

        @pl.when(i >= first)
        def _():
            hbuf[s] = jnp.zeros(hbuf.shape[1:], hbuf.dtype)

        def issue(t, carry):
            for k in range(TOP_K):
                _rows_copy(hbuf.at[s], t * rpt, xs_ref, dest_ref[t * TOP_K + k] * rpt, rpt,
                           sems.at[s]).start(priority=k % 2)
            return carry

        lax.fori_loop(0, tm, issue, 0)

        @pl.when(i == n_steps - 1)
        def _():
            drain(s)

        @pl.when(jnp.logical_and(i == n_steps - 1, i >= 1))
        def _():
            drain(1 - s)

    _for_slot(i % 2, step)


def _dispatch(h2ps, dest_all, rpt):
    tm = ROW_TM
    assert all(h.shape[1] == LANES and h.shape[0] % (tm * rpt) == 0 for h in h2ps)
    tiles = tuple(h.shape[0] // (tm * rpt) for h in h2ps)
    rows = dest_all.shape[0]
    n_steps = rows // (tm * TOP_K)
    in_specs = [pl.BlockSpec((tm * TOP_K,), lambda i: (i,), memory_space=pltpu.SMEM)]
    first = 0
    for cnt in tiles:
        in_specs.append(pl.BlockSpec((tm * rpt, LANES),
                                     lambda i, first=first, cnt=cnt: (jnp.clip(i - first, 0, cnt - 1), 0)))
        first += cnt
    return pl.pallas_call(
        functools.partial(_dispatch_kernel, tiles=tiles, rpt=rpt),
        grid=(n_steps,),
        in_specs=in_specs,
        out_specs=pl.BlockSpec(memory_space=pl.ANY),
        out_shape=jax.ShapeDtypeStruct((rows * rpt, LANES), jnp.uint32),
        scratch_shapes=[pltpu.VMEM((2, tm * rpt, LANES), jnp.uint32), pltpu.SemaphoreType.DMA((2,))],
        compiler_params=_params("arbitrary"),
        name="dispatch",
    )(dest_all, *h2ps)


def _combine_kernel(dcur_ref, dnext_ref, x1_ref, gw_ref, g2_ref, fg_ref, ys_ref, o_ref, buf, sems):
    tm = x1_ref.shape[0]
    rpt = buf.shape[2] // tm
    i = pl.program_id(0)
    n_steps = pl.num_programs(0)

    def issue(d_ref, s):
        def body(t, carry):
            for k in range(TOP_K):
                _rows_copy(ys_ref, d_ref[t * TOP_K + k] * rpt, buf.at[s, k], t * rpt, rpt,
                           sems.at[s]).start(priority=k % 2)
            return carry
        lax.fori_loop(0, tm, body, 0)

    def step(s):
        @pl.when(i == 0)
        def _():
            issue(dcur_ref, s)

        @pl.when(i + 1 < n_steps)
        def _():
            issue(dnext_ref, 1 - s)

        def drain(t, carry):
            for k in range(TOP_K):
                _rows_copy(ys_ref, 0, buf.at[s, k], 0, rpt, sems.at[s]).wait()
            return carry

        lax.fori_loop(0, tm, drain, 0)
        gw = gw_ref[...]
        y = None
        for k in range(TOP_K):
            yk = gw[:, k:k + 1] * _unpack_bf16_pairs(_load_tokens(buf.at[s, k], tm)).astype(F32)
            y = yk if y is None else y + yk
        x2 = x1_ref[...] + g2_ref[0] * y
        o_ref[...] = _rms(x2, fg_ref[...])

    _for_slot(i % 2, step)


def _combine(x1, gw, g2, final_g, ys, dest, seq):
    n, d = x1.shape
    tm = _tile(seq, ROW_TM)
    tps = seq // tm
    last = n // tm - 1
    return pl.pallas_call(
        _combine_kernel,
        grid=(n // tm,),
        in_specs=[pl.BlockSpec((tm * TOP_K,), lambda i: (i,), memory_space=pltpu.SMEM),
                  pl.BlockSpec((tm * TOP_K,), lambda i: (jnp.minimum(i + 1, last),), memory_space=pltpu.SMEM),
                  pl.BlockSpec((tm, d), lambda i: (i, 0)),
                  pl.BlockSpec((tm, LANES), lambda i: (i, 0)),
                  pl.BlockSpec((1, 1, d), lambda i: (i // tps, 0, 0)),
                  pl.BlockSpec((1, d), lambda i: (0, 0)),
                  pl.BlockSpec(memory_space=pl.ANY)],
        out_specs=pl.BlockSpec((tm, d), lambda i: (i, 0)),
        out_shape=jax.ShapeDtypeStruct((n, d), F32),
        scratch_shapes=[pltpu.VMEM((2, TOP_K, tm * (d // 2 // LANES), LANES), jnp.uint32),
                        pltpu.SemaphoreType.DMA((2,))],
        compiler_params=_params("arbitrary"),
        name="combine",
    )(dest, dest, x1, gw, g2, final_g.reshape(1, d), ys)


def _load_tokens(ref, n_tok, first=0, rpt=None):
    if rpt is None:
        rpt = ref.shape[0] // n_tok
    return jnp.concatenate([ref[pl.ds(first * rpt + s, n_tok, stride=rpt), :] for s in range(rpt)], axis=1)


def _store_tokens(ref, x):
    n_tok = x.shape[0]
    rpt = x.shape[1] // LANES
    for s in range(rpt):
        ref[pl.ds(s, n_tok, stride=rpt), :] = x[:, s * LANES:(s + 1) * LANES]


def _moe_kernel(be_ref, bv_ref, x_ref, wg_ref, wu_ref, wd_ref, bg_ref, bu_ref, bd_ref, o_ref, xb_scr, acc):
    del be_ref
    n_valid = bv_ref[pl.program_id(0)]
    f = pl.program_id(1)
    bm = xb_scr.shape[0]

    @pl.when(f == 0)
    def _():
        acc[...] = jnp.broadcast_to(bd_ref[0], acc.shape)

    @pl.when(jnp.logical_and(n_valid > 0, f == 0))
    def _():
        xb_scr[...] = _unpack_bf16_pairs(_load_tokens(x_ref, bm))

    def experts(rows):
        xb = xb_scr[0:rows, :]
        gate = jnp.dot(xb, wg_ref[0], preferred_element_type=F32) + bg_ref[0]
        up = jnp.dot(xb, wu_ref[0], preferred_element_type=F32) + bu_ref[0]
        gate = jnp.minimum(gate, SWIGLU_LIMIT)
        up = jnp.clip(up, -SWIGLU_LIMIT, SWIGLU_LIMIT)
        act = (up + 1.0) * (gate * jax.nn.sigmoid(SWIGLU_ALPHA * gate))
        acc[0:rows, :] += jnp.dot(act.astype(BF16), wd_ref[0], preferred_element_type=F32)

    pl.when(n_valid > bm // 2)(functools.partial(experts, bm))
    pl.when(jnp.logical_and(n_valid > 0, n_valid <= bm // 2))(functools.partial(experts, bm // 2))

    @pl.when(f == pl.num_programs(1) - 1)
    def _():
        _store_tokens(o_ref, _pack_bf16_pairs(acc[...].astype(BF16)))


def _moe(xs, block_e, block_valid, w_gu, b_gu, w_dn, b_dn):
    n_exp, d_ff, d = w_dn.shape
    rpt = d // 2 // LANES
    rows = xs.shape[0] // rpt
    fc = _tile(d_ff, 1024)
    nf = d_ff // fc
    n_blocks = rows // MOE_BM
    grid_spec = pltpu.PrefetchScalarGridSpec(
        num_scalar_prefetch=2,
        grid=(n_blocks, nf),
        in_specs=[pl.BlockSpec((MOE_BM * rpt, LANES), lambda i, f, be, bv: (i, 0)),
                  pl.BlockSpec((1, d, fc), lambda i, f, be, bv: (be[i], 0, f)),
                  pl.BlockSpec((1, d, fc), lambda i, f, be, bv: (be[i], 0, nf + f)),
                  pl.BlockSpec((1, fc, d), lambda i, f, be, bv: (be[i], f, 0)),
                  pl.BlockSpec((1, 1, fc), lambda i, f, be, bv: (be[i], 0, f)),
                  pl.BlockSpec((1, 1, fc), lambda i, f, be, bv: (be[i], 0, nf + f)),
                  pl.BlockSpec((1, 1, d), lambda i, f, be, bv: (be[i], 0, 0))],
        out_specs=pl.BlockSpec((MOE_BM * rpt, LANES), lambda i, f, be, bv: (i, 0)),
        scratch_shapes=[pltpu.VMEM((MOE_BM, d), BF16), pltpu.VMEM((MOE_BM, d), F32)],
    )
    return pl.pallas_call(
        _moe_kernel,
        grid_spec=grid_spec,
        out_shape=jax.ShapeDtypeStruct((rows * rpt, LANES), jnp.uint32),
        compiler_params=_params("arbitrary", "arbitrary"),
        name="moe",
    )(block_e, block_valid, xs, w_gu, w_gu, w_dn,
      b_gu.reshape(n_exp, 1, 2 * d_ff), b_gu.reshape(n_exp, 1, 2 * d_ff), b_dn.reshape(n_exp, 1, d))


def _mixer_and_router(x, ada, wts, base0):
    (ln1_g, ln2_g, w_main, w_gates, b_gates, conv_w, conv_b, head_norm_g, w_four, w_out_b, mc,
     wr_cat, b_router) = wts
    bsz, seq, d = x.shape
    n = bsz * seq
    mw = d // 2
    dh = mw // N_HEADS
    x2 = x.reshape(n, d)
    sh1, sc1, g1, sh2, sc2, g2 = [a.reshape(bsz, 1, d) for a in jnp.split(ada, 6, axis=-1)]

    proj, gates = _inproj(x2, ln1_g, sc1, sh1, w_main, w_gates, b_gates, seq)
    q, kt = _conv(proj, conv_w, conv_b, seq, mw, dh ** -0.5)
    gates_t = gates[:, :N_GATES].reshape(bsz, seq, N_GATES).transpose(0, 2, 1).reshape(bsz * N_GATES, seq)
    hf, hb = _mlstm(q, kt, proj, gates, gates_t, bsz, seq, mw)
    assert d - mw == mw, "the Fourier block is addressed as column block 4 of proj"
    pdft = _position_dft(proj, bsz, seq, mw, 4)
    x1, h2p, idx, gw, rank, cnt = _mix(hf, hb, proj, pdft, x2, g1, sc2, sh2, head_norm_g, ln2_g,
                                       w_out_b, mc, w_four, wr_cat, b_router, base0, seq)
    return dict(x1=x1, h2p=h2p, idx=idx[:, :TOP_K], gw=gw, rank=rank[:, :TOP_K].astype(jnp.int32),
                cnt=cnt, g2=g2, shape=(bsz, seq, d))


def kernel(x_prompt, x_sample, c_prompt, c_sample, ln1_g, ln2_g, w_ada, b_ada, w_in, b_gates, conv_w,
           conv_b, head_norm_g, w_four, w_out, w_router, b_router, w_gate_up, b_gate_up, w_down, b_down,
           final_g):
    assert w_ada.shape[0] == 1, "single-layer trunk"
    d = x_prompt.shape[-1]
    mw = d // 2
    cg = (d - mw) // N_GROUPS
    w_in0 = w_in[0]
    w_main = jnp.concatenate([w_in0[:, :4 * mw], w_in0[:, 4 * mw + N_GATES:]], axis=1).astype(BF16)
    w_gates = jnp.pad(w_in0[:, 4 * mw:4 * mw + N_GATES], ((0, 0), (0, LANES - N_GATES))).astype(BF16)
    bg = jnp.pad(b_gates[0], (0, LANES - N_GATES)).reshape(1, LANES)
    kc = np.arange(cg)
    angc = 2.0 * np.pi * np.outer(kc, kc) / cg
    mc = jnp.asarray(np.concatenate([np.cos(angc), -np.sin(angc)], axis=0) / math.sqrt(cg), BF16)
    wr = w_router[0]
    wr_hi = wr.astype(BF16)
    wr_lo = (wr - wr_hi.astype(F32)).astype(BF16)
    n_exp = wr.shape[1]
    mxu_cols = 2 * LANES
    wr_cat = jnp.pad(jnp.concatenate([wr_hi, wr_lo], axis=1), ((0, 0), (0, max(0, mxu_cols - 2 * n_exp))))
    wts = (ln1_g[0], ln2_g[0], w_main, w_gates, bg, conv_w[0], conv_b[0], head_norm_g[0], w_four[0],
           w_out[0].astype(BF16), mc, wr_cat, b_router[0])
    nbp = c_prompt.shape[0]
    ada = _ada(jnp.concatenate([c_prompt, c_sample], axis=0), w_ada[0], b_ada[0])
    gp = _mixer_and_router(x_prompt, ada[:nbp], wts, jnp.zeros((1, n_exp), F32))
    gs = _mixer_and_router(x_sample, ada[nbp:], wts, gp["cnt"])
    groups = (gp, gs)

    counts = gs["cnt"][0].astype(jnp.int32)
    p_counts = (counts + MOE_BM - 1) // MOE_BM * MOE_BM
    p_ends = jnp.cumsum(p_counts)
    p_off = p_ends - p_counts
    n_tok = sum(g["x1"].shape[0] for g in groups)
    n_blocks = (n_tok * TOP_K) // MOE_BM + n_exp
    starts = jnp.arange(n_blocks, dtype=jnp.int32) * MOE_BM
    block_e = jnp.minimum(jnp.sum((starts[:, None] >= p_ends[None, :]).astype(jnp.int32), axis=1), n_exp - 1)
    block_valid = jnp.clip((p_off + counts)[block_e] - starts, 0, MOE_BM).astype(jnp.int32)
    dests = [(p_off[g["idx"]] + g["rank"]).reshape(-1) for g in groups]
    rows = n_blocks * MOE_BM
    gap_start = jnp.concatenate([p_off + counts, p_ends[-1:]])
    gap_len = jnp.concatenate([p_counts - counts, rows - p_ends[-1:]])
    gap_end = jnp.cumsum(gap_len)
    j = jnp.arange(n_exp * MOE_BM, dtype=jnp.int32)
    gi = jnp.sum((j[:, None] >= gap_end[None, :]).astype(jnp.int32), axis=1)
    pad_dest = (gap_start[gi] + (j - (gap_end - gap_len)[gi])).astype(jnp.int32)

    xs = _dispatch([g["h2p"] for g in groups], jnp.concatenate(dests + [pad_dest]), d // 2 // LANES)
    ys = _moe(xs, block_e, block_valid, w_gate_up[0].astype(BF16), b_gate_up[0], w_down[0].astype(BF16), b_down[0])
    outs = []
    for g, dest in zip(groups, dests):
        bsz, seq, _ = g["shape"]
        outs.append(_combine(g["x1"], g["gw"], g["g2"], final_g, ys, dest, seq).reshape(g["shape"]))
    return tuple(outs)
```

```python
import functools
import math

import numpy as np
import jax
import jax.numpy as jnp
from jax import lax
from jax.experimental import pallas as pl
from jax.experimental.pallas import tpu as pltpu

F32 = jnp.float32
BF16 = jnp.bfloat16

N_HEADS = 4
N_GROUPS = 4
N_GATES = 4 * N_HEADS
CHUNK = 128
TOP_K = 4
SWIGLU_LIMIT = 7.0
SWIGLU_ALPHA = 1.702
EPS = 1e-6
LANES = 128
SUBLANES = 8
DFT_N2 = 128
MOE_BM = 512
ROW_TM = 256
VMEM_LIMIT = 56 * 1024 * 1024


def _params(*sem, flags=None):
    return pltpu.CompilerParams(dimension_semantics=sem, vmem_limit_bytes=VMEM_LIMIT, flags=flags)


def _tile(n, pref):
    t = min(n, pref)
    while n % t:
        t //= 2
    return t


def _ada_kernel(c_ref, w_ref, b_ref, o_ref):
    c = c_ref[...]
    s = (c * jax.nn.sigmoid(c)).astype(BF16)
    o_ref[...] = jnp.dot(s, w_ref[...].astype(BF16), preferred_element_type=F32) + b_ref[...]


def _ada(c, w_ada, b_ada):
    bsz, d = c.shape
    bp = -(-bsz // SUBLANES) * SUBLANES
    cp = jnp.pad(c, ((0, bp - bsz), (0, 0)))
    n_out = w_ada.shape[1]
    tn = _tile(n_out, 1024)
    out = pl.pallas_call(
        _ada_kernel,
        grid=(n_out // tn,),
        in_specs=[pl.BlockSpec((bp, d), lambda j: (0, 0)),
                  pl.BlockSpec((d, tn), lambda j: (0, j)),
                  pl.BlockSpec((1, tn), lambda j: (0, j))],
        out_specs=pl.BlockSpec((bp, tn), lambda j: (0, j)),
        out_shape=jax.ShapeDtypeStruct((bp, n_out), F32),
        compiler_params=_params("arbitrary"),
        name="ada",
    )(cp, w_ada, b_ada.reshape(1, n_out))
    return out[:bsz]


def _rms(x, g):
    return (x * lax.rsqrt(jnp.mean(x * x, axis=-1, keepdims=True) + EPS)) * g


def _inproj_kernel(x_ref, g_ref, sc_ref, sh_ref, w_ref, wg_ref, bg_ref, o_ref, og_ref, h_scr):
    @pl.when(pl.program_id(1) == 0)
    def _():
        h = _rms(x_ref[...], g_ref[...]) * (1.0 + sc_ref[0]) + sh_ref[0]
        hb = h.astype(BF16)
        h_scr[...] = hb
        og_ref[...] = jnp.dot(hb, wg_ref[...], preferred_element_type=F32) + bg_ref[...]

    o_ref[...] = jnp.dot(h_scr[...], w_ref[...], preferred_element_type=F32)


def _inproj(x2, ln_g, sc, sh, w_main, w_gates, b_gates, seq):
    n, d = x2.shape
    p = w_main.shape[1]
    tm = _tile(seq, 1024)
    tn = _tile(p, 1024)
    tps = seq // tm
    return pl.pallas_call(
        _inproj_kernel,
        grid=(n // tm, p // tn),
        in_specs=[pl.BlockSpec((tm, d), lambda i, j: (i, 0)),
                  pl.BlockSpec((1, d), lambda i, j: (0, 0)),
                  pl.BlockSpec((1, 1, d), lambda i, j: (i // tps, 0, 0)),
                  pl.BlockSpec((1, 1, d), lambda i, j: (i // tps, 0, 0)),
                  pl.BlockSpec((d, tn), lambda i, j: (0, j)),
                  pl.BlockSpec((d, LANES), lambda i, j: (0, 0)),
                  pl.BlockSpec((1, LANES), lambda i, j: (0, 0))],
        out_specs=[pl.BlockSpec((tm, tn), lambda i, j: (i, j)),
                   pl.BlockSpec((tm, LANES), lambda i, j: (i, 0))],
        out_shape=[jax.ShapeDtypeStruct((n, p), F32),
                   jax.ShapeDtypeStruct((n, LANES), F32)],
        scratch_shapes=[pltpu.VMEM((tm, d), BF16)],
        compiler_params=_params("arbitrary", "arbitrary"),
        name="inproj",
    )(x2, ln_g.reshape(1, d), sc, sh, w_main, w_gates, b_gates)


def _conv_kernel(x_ref, prev_ref, next_ref, w_ref, b_ref, q_ref, kt_ref, res_scr, *, tps, k_scale):
    it = pl.program_id(0) % tps
    w = w_ref[...]
    b = b_ref[...]
    taps = w.shape[0]
    half = taps // 2

    def conv_act(z):
        nz = z.shape[0]
        acc = z * w[half:half + 1]
        for j in range(taps):
            if j != half:
                acc = acc + pltpu.roll(z, (half - j) % nz, 0) * w[j:j + 1]
        acc = acc + b
        return acc * jax.nn.sigmoid(acc)

    x = x_ref[...]
    tm = x.shape[0]
    res_scr[...] = conv_act(x)
    prev = jnp.where(it == 0, 0.0, prev_ref[...])
    nxt = jnp.where(it == tps - 1, 0.0, next_ref[...])
    top = conv_act(jnp.concatenate([prev, x[0:2 * SUBLANES]], axis=0))
    res_scr[0:SUBLANES, :] = top[SUBLANES:2 * SUBLANES]
    bot = conv_act(jnp.concatenate([x[tm - 2 * SUBLANES:tm], nxt], axis=0))
    res_scr[tm - SUBLANES:tm, :] = bot[SUBLANES:2 * SUBLANES]

    @pl.when(pl.program_id(1) == 0)
    def _():
        q_ref[...] = res_scr[...]

    @pl.when(pl.program_id(1) == 1)
    def _():
        for r in range(0, tm, LANES):
            kt_ref[:, r:r + LANES] = (res_scr[r:r + LANES, :] * k_scale).T


def _conv(proj, conv_w, conv_b, seq, mw, k_scale):
    n = proj.shape[0]
    taps = conv_w.shape[0]
    assert taps // 2 <= SUBLANES
    tm = _tile(seq, 512)
    assert tm >= 4 * SUBLANES
    tps = seq // tm
    r8 = tm // SUBLANES
    last8 = n // SUBLANES - 1
    return pl.pallas_call(
        functools.partial(_conv_kernel, tps=tps, k_scale=k_scale),
        grid=(n // tm, 2),
        in_specs=[pl.BlockSpec((tm, mw), lambda i, j: (i, j)),
                  pl.BlockSpec((SUBLANES, mw), lambda i, j: (jnp.maximum(i * r8 - 1, 0), j)),
                  pl.BlockSpec((SUBLANES, mw), lambda i, j: (jnp.minimum((i + 1) * r8, last8), j)),
                  pl.BlockSpec((taps, mw), lambda i, j: (0, j)),
                  pl.BlockSpec((1, mw), lambda i, j: (0, j))],
        out_specs=[pl.BlockSpec((tm, mw), lambda i, j: (i, 0)),
                   pl.BlockSpec((mw, tm), lambda i, j: (0, i))],
        out_shape=[jax.ShapeDtypeStruct((n, mw), F32), jax.ShapeDtypeStruct((mw, n), F32)],
        scratch_shapes=[pltpu.VMEM((tm, mw), F32)],
        compiler_params=_params("arbitrary", "arbitrary"),
        name="conv",
    )(proj, proj, proj, conv_w, conv_b.reshape(1, 2 * mw))


def _log_sigmoid(x):
    return jnp.minimum(x, 0.0) - jnp.log1p(jnp.exp(-jnp.abs(x)))


def _split3(x):
    hi = x.astype(BF16)
    r = x - hi.astype(F32)
    mid = r.astype(BF16)
    lo = (r - mid.astype(F32)).astype(BF16)
    return hi, mid, lo


def _cummax(x, axis, reverse):
    n = x.shape[axis]
    pos = lax.broadcasted_iota(jnp.int32, x.shape, axis)
    step = 1
    while step < n:
        if reverse:
            shifted = jnp.where(pos < n - step, pltpu.roll(x, n - step, axis), -jnp.inf)
        else:
            shifted = jnp.where(pos >= step, pltpu.roll(x, step, axis), -jnp.inf)
        x = jnp.maximum(x, shifted)
        step *= 2
    return x


def _mlstm_gates(gc, gr, tri_c, tri_r, m_lane, m_sub, reverse):
    length = gc.shape[0]
    last = 0 if reverse else length - 1
    b_cols = jnp.dot(tri_c, jnp.concatenate(_split3(_log_sigmoid(gc)), axis=0), preferred_element_type=F32)
    b_rows = jnp.dot(jnp.concatenate(_split3(_log_sigmoid(gr)), axis=1), tri_r, preferred_element_type=F32)
    b_cols = pltpu.roll(b_cols, LANES - N_HEADS, 1)
    b_rows = pltpu.roll(b_rows, N_GATES - N_HEADS, 0)
    u_cols = gc - b_cols
    u_rows = gr - b_rows
    c_cols = jnp.maximum(m_lane, _cummax(u_cols, 0, reverse))
    c_rows = jnp.maximum(m_sub, _cummax(u_rows, 1, reverse))
    c_last = c_rows[:, last:last + 1]
    out = dict(
        c_cols=c_cols, u_rows=u_rows,
        ea_cols=jnp.exp(m_lane - c_cols),
        floor_cols=jnp.exp(-b_cols - c_cols),
        ew_rows=jnp.exp(u_rows - c_last),
        decay_rows=jnp.exp(m_sub - c_last),
        m_lane=b_cols[last:last + 1, :] + c_cols[last:last + 1, :],
        m_sub=jnp.broadcast_to(b_rows[:, last:last + 1] + c_last, m_sub.shape),
    )
    return out


def _mlstm_heads(work, gate_fn, cn_ref):
    length = work[0][0].shape[0]
    row = lax.broadcasted_iota(jnp.int32, (length, length), 0)
    col = lax.broadcasted_iota(jnp.int32, (length, length), 1)
    qxs = []
    for q_ref, kt_ref, v_ref, h_ref, sl, di, ci, idx, reverse in work:
        rhs = jnp.concatenate([kt_ref[sl, :].astype(BF16), cn_ref[idx].astype(BF16)], axis=1)
        qxs.append(jnp.dot(q_ref[:, sl].astype(BF16), rhs, preferred_element_type=F32))
    gts = gate_fn()
    work = [w[:5] + (gts[w[5]],) + w[6:] for w in work]
    wms, lhs, wvs = [], [], []
    for (q_ref, kt_ref, v_ref, h_ref, sl, gt, ci, idx, reverse), qx in zip(work, qxs):
        mask = (col >= row) if reverse else (col <= row)
        wm = jnp.exp(jnp.where(mask, gt["u_rows"][ci:ci + 1, :] - gt["c_cols"][:, ci:ci + 1], -jnp.inf))
        wm = wm * qx[:, :length]
        kw = kt_ref[sl, :] * gt["ew_rows"][ci:ci + 1, :]
        wms.append(wm)
        lhs.append(jnp.concatenate([wm, kw], axis=0).astype(BF16))
    for (q_ref, kt_ref, v_ref, h_ref, sl, gt, ci, idx, reverse), lh in zip(work, lhs):
        v1 = jnp.concatenate([v_ref[:, sl].astype(BF16), jnp.ones((length, LANES), BF16)], axis=1)
        wvs.append(jnp.dot(lh, v1, preferred_element_type=F32))
    for (q_ref, kt_ref, v_ref, h_ref, sl, gt, ci, idx, reverse), qx, wm, wv in zip(work, qxs, wms, wvs):
        dh = wv.shape[1] - LANES
        ea = gt["ea_cols"][:, ci:ci + 1]
        num = wv[:length, :dh] + ea * qx[:, length:length + dh]
        den = jnp.sum(wm, axis=1, keepdims=True) + ea * qx[:, length + dh:length + dh + 1]
        h_ref[:, sl] = num / jnp.maximum(jnp.abs(den), gt["floor_cols"][:, ci:ci + 1])
        width = wv.shape[1]
        decay = jnp.concatenate([gt["decay_rows"][ci:ci + 1, :]] * pl.cdiv(width, LANES), axis=1)[:, :width]
        cn_ref[idx] = decay * cn_ref[idx] + wv[length:, :]


def _mlstm_kernel(qf_ref, kf_ref, vf_ref, gcf_ref, grf_ref,
                  qb_ref, kb_ref, vb_ref, gcb_ref, grb_ref,
                  l3_ref, u3_ref, l3t_ref, u3t_ref,
                  hf_ref, hb_ref, cn_scr, ml_scr, ms_scr):
    @pl.when(pl.program_id(1) == 0)
    def _():
        cn_scr[...] = jnp.zeros_like(cn_scr)
        ml_scr[...] = jnp.zeros_like(ml_scr)
        ms_scr[...] = jnp.zeros_like(ms_scr)

    dh = qf_ref.shape[1] // N_HEADS
    dirs = (
        (False, qf_ref, kf_ref, vf_ref, gcf_ref, grf_ref, l3_ref, u3t_ref, hf_ref),
        (True, qb_ref, kb_ref, vb_ref, gcb_ref, grb_ref, u3_ref, l3t_ref, hb_ref),
    )
    work = []
    for di, (reverse, q_ref, kt_ref, v_ref, gc_ref, gr_ref, tri_c, tri_r, h_ref) in enumerate(dirs):
        for hd in range(N_HEADS):
            sl = slice(hd * dh, (hd + 1) * dh)
            work.append((q_ref, kt_ref, v_ref, h_ref, sl, di, di * 2 * N_HEADS + hd, di * N_HEADS + hd, reverse))

    def gate_fn():
        gts = []
        for di, (reverse, q_ref, kt_ref, v_ref, gc_ref, gr_ref, tri_c, tri_r, h_ref) in enumerate(dirs):
            gt = _mlstm_gates(gc_ref[...], gr_ref[...], tri_c[...], tri_r[...], ml_scr[di], ms_scr[di], reverse)
            ml_scr[di] = gt["m_lane"]
            ms_scr[di] = gt["m_sub"]
            gts.append(gt)
        return gts

    _mlstm_heads(work, gate_fn, cn_scr)


def _tri_consts(length):
    lower = np.tril(np.ones((length, length), np.float32))
    upper = lower.T
    l3 = np.concatenate([lower] * 3, axis=1)
    u3 = np.concatenate([upper] * 3, axis=1)
    l3t = np.concatenate([lower] * 3, axis=0)
    u3t = np.concatenate([upper] * 3, axis=0)
    return tuple(jnp.asarray(a, BF16) for a in (l3, u3, l3t, u3t))


def _mlstm(q, kt, proj, gates, gates_t, bsz, seq, mw):
    n = q.shape[0]
    nc = seq // CHUNK
    dh = mw // N_HEADS
    assert CHUNK == LANES
    fwd = lambda b, c: b * nc + c
    bwd = lambda b, c: b * nc + (nc - 1 - c)
    cst = lambda b, c: (0, 0)
    in_specs = []
    for pos in (fwd, bwd):
        in_specs += [
            pl.BlockSpec((CHUNK, mw), lambda b, c, pos=pos: (pos(b, c), 0)),
            pl.BlockSpec((mw, CHUNK), lambda b, c, pos=pos: (0, pos(b, c))),
            pl.BlockSpec((CHUNK, mw), lambda b, c, pos=pos: (pos(b, c), 2)),
            pl.BlockSpec((CHUNK, LANES), lambda b, c, pos=pos: (pos(b, c), 0)),
            pl.BlockSpec((N_GATES, CHUNK), lambda b, c, pos=pos: (b, pos(0, c))),
        ]
    in_specs += [pl.BlockSpec((CHUNK, 3 * CHUNK), cst), pl.BlockSpec((CHUNK, 3 * CHUNK), cst),
                 pl.BlockSpec((3 * CHUNK, CHUNK), cst), pl.BlockSpec((3 * CHUNK, CHUNK), cst)]
    return pl.pallas_call(
        _mlstm_kernel,
        grid=(bsz, nc),
        in_specs=in_specs,
        out_specs=[pl.BlockSpec((CHUNK, mw), lambda b, c: (fwd(b, c), 0)),
                   pl.BlockSpec((CHUNK, mw), lambda b, c: (bwd(b, c), 0))],
        out_shape=[jax.ShapeDtypeStruct((n, mw), F32), jax.ShapeDtypeStruct((n, mw), F32)],
        scratch_shapes=[pltpu.VMEM((2 * N_HEADS, dh, dh + LANES), F32),
                        pltpu.VMEM((2, 1, LANES), F32),
                        pltpu.VMEM((2, N_GATES, LANES), F32)],
        compiler_params=_params("arbitrary", "arbitrary"),
        name="mlstm",
    )(q, kt, proj, gates, gates_t, q, kt, proj, gates, gates_t, *_tri_consts(CHUNK))


def _dft_a_kernel(x_ref, ga_ref, tc_ref, ts_ref, o_ref, *, jb):
    n1, width = x_ref.shape[1], x_ref.shape[3]
    rows = n1 * SUBLANES
    reps = width // LANES
    ga = ga_ref[...]
    for s2 in range(jb // 2):
        cos_parts, sin_parts = [], []
        for s in (2 * s2, 2 * s2 + 1):
            sl = slice(s * SUBLANES, (s + 1) * SUBLANES)
            x = x_ref[0, :, sl, :].reshape(rows, width).astype(BF16)
            z = jnp.dot(ga, x, preferred_element_type=F32)
            zc, zs = z[:rows], z[rows:]
            tc = jnp.concatenate([tc_ref[s]] * reps, axis=1)
            ts = jnp.concatenate([ts_ref[s]] * reps, axis=1)
            cos_parts.append((zc * tc - zs * ts).reshape(n1, SUBLANES, width))
            sin_parts.append((zc * ts + zs * tc).reshape(n1, SUBLANES, width))
        sl2 = slice(2 * s2 * SUBLANES, (2 * s2 + 2) * SUBLANES)
        o_ref[0, 0, :, sl2, :] = jnp.concatenate(cos_parts, axis=1).astype(BF16)
        o_ref[0, 1, :, sl2, :] = jnp.concatenate(sin_parts, axis=1).astype(BF16)


def _dft_b_kernel(z_ref, gb_ref, o_ref, scr):
    n2, wb = z_ref.shape[3], z_ref.shape[4]
    gb = gb_ref[...]
    for k in range(SUBLANES):
        slab = z_ref[0, :, k].reshape(2 * n2, wb)
        p = jnp.dot(gb, slab, preferred_element_type=F32)
        for cs in range(2):
            for c in range(wb // LANES):
                scr[c, pl.ds(cs * n2 * SUBLANES + k, n2, stride=SUBLANES), :] = (
                    p[cs * n2:(cs + 1) * n2, c * LANES:(c + 1) * LANES])
    for c in range(wb // LANES):
        o_ref[0, :, :, :, c * LANES:(c + 1) * LANES] = scr[c].reshape(2, n2, SUBLANES, LANES)


def _position_dft(proj, bsz, seq, width, col_block):
    n2 = DFT_N2
    n1 = seq // n2
    assert seq % n2 == 0 and n1 % SUBLANES == 0 and proj.shape[1] % width == 0
    rows = n1 * SUBLANES
    k1 = np.arange(n1)
    ang1 = 2.0 * np.pi * np.outer(k1, k1) / n1
    eye = np.eye(SUBLANES)
    ga = jnp.asarray(np.concatenate([np.kron(np.cos(ang1), eye), np.kron(np.sin(ang1), eye)], axis=0)
                     / math.sqrt(n1), BF16)
    k2 = np.arange(n2)
    ang2 = 2.0 * np.pi * np.outer(k2, k2) / n2
    c2, s2 = np.cos(ang2), np.sin(ang2)
    gb = jnp.asarray(np.block([[c2, -s2], [s2, c2]]) / math.sqrt(n2), BF16)
    n2_idx = jnp.arange(n2, dtype=jnp.int32).reshape(n2 // SUBLANES, 1, SUBLANES)
    k1_idx = jnp.arange(n1, dtype=jnp.int32).reshape(1, n1, 1)
    ang = ((2.0 * math.pi / seq) * ((n2_idx * k1_idx) % seq).astype(F32)).reshape(n2 // SUBLANES, rows)
    tc = jnp.broadcast_to(jnp.cos(ang)[:, :, None], (n2 // SUBLANES, rows, LANES))
    ts = jnp.broadcast_to(jnp.sin(ang)[:, :, None], (n2 // SUBLANES, rows, LANES))

    jb = max(2, min(n2 // SUBLANES, 512 // rows))
    wa = _tile(width, 512)
    wpa = width // wa
    proj4 = proj.reshape(bsz, n1, n2, proj.shape[1])
    z = pl.pallas_call(
        functools.partial(_dft_a_kernel, jb=jb),
        grid=(bsz, n2 // (SUBLANES * jb), wpa),
        in_specs=[pl.BlockSpec((1, n1, SUBLANES * jb, wa), lambda b, j, w: (b, 0, j, col_block * wpa + w)),
                  pl.BlockSpec((2 * rows, rows), lambda b, j, w: (0, 0)),
                  pl.BlockSpec((jb, rows, LANES), lambda b, j, w: (j, 0, 0)),
                  pl.BlockSpec((jb, rows, LANES), lambda b, j, w: (j, 0, 0))],
        out_specs=pl.BlockSpec((1, 2, n1, SUBLANES * jb, wa), lambda b, j, w: (b, 0, 0, j, w)),
        out_shape=jax.ShapeDtypeStruct((bsz, 2, n1, n2, width), BF16),
        compiler_params=_params("arbitrary", "arbitrary", "arbitrary"),
        name="dft_a",
    )(proj4, ga, tc, ts)
    wb = _tile(width, 512)
    p = pl.pallas_call(
        _dft_b_kernel,
        grid=(bsz, n1 // SUBLANES, width // wb),
        in_specs=[pl.BlockSpec((1, 2, SUBLANES, n2, wb), lambda b, j, w: (b, 0, j, 0, w)),
                  pl.BlockSpec((2 * n2, 2 * n2), lambda b, j, w: (0, 0))],
        out_specs=pl.BlockSpec((1, 2, n2, SUBLANES, wb), lambda b, j, w: (b, 0, 0, j, w)),
        out_shape=jax.ShapeDtypeStruct((bsz, 2, n2, n1, width), F32),
        scratch_shapes=[pltpu.VMEM((wb // LANES, 2 * n2 * SUBLANES, LANES), F32)],
        compiler_params=_params("arbitrary", "arbitrary", "arbitrary"),
        name="dft_b",
    )(z, gb)
    return p.reshape(bsz, 2, seq, width)


def _pack_bf16_pairs(xb):
    c = xb.shape[1] // 2
    lo = pltpu.bitcast(xb[:, :c].astype(F32), jnp.uint32)
    hi = pltpu.bitcast(xb[:, c:].astype(F32), jnp.uint32)
    return (lo >> 16) | hi


def _unpack_bf16_pairs(w):
    lo = pltpu.bitcast(w << 16, F32).astype(BF16)
    hi = pltpu.bitcast(w & jnp.uint32(0xFFFF0000), F32).astype(BF16)
    return jnp.concatenate([lo, hi], axis=1)


def _mix_kernel(hf_ref, hb_ref, op_ref, pc_ref, ps_ref, x_ref, g1_ref, sc_ref, sh_ref,
                hng_ref, ln_ref, wout_ref, mc_ref, wf_ref, wrh_ref, br_ref, ltri_ref, base0_ref,
                x1_ref, h2_ref, idx_ref, gw_ref, rank_ref, cnt_ref, base_scr):
    @pl.when(pl.program_id(0) == 0)
    def _():
        base_scr[...] = base0_ref[...]

    tm, mw = hf_ref.shape
    dh = mw // N_HEADS
    hm = hf_ref[...] + hb_ref[...]
    parts = []
    for hd in range(N_HEADS):
        seg = hm[:, hd * dh:(hd + 1) * dh]
        parts.append(seg * lax.rsqrt(jnp.mean(seg * seg, axis=-1, keepdims=True) + EPS))
    hm = jnp.concatenate(parts, axis=1) * hng_ref[...] * jax.nn.sigmoid(op_ref[...])
    cg = pc_ref.shape[3] // N_GROUPS
    pc = pc_ref[0, 0]
    ps = ps_ref[0, 0]
    specs = []
    for gi in range(N_GROUPS):
        sl = slice(gi * cg, (gi + 1) * cg)
        specs.append(jnp.dot(jnp.concatenate([pc[:, sl], ps[:, sl]], axis=1).astype(BF16), mc_ref[...],
                             preferred_element_type=F32))
    for gi in range(N_GROUPS):
        parts.append(jnp.dot(specs[gi].astype(BF16), wf_ref[gi].astype(BF16), preferred_element_type=F32))
    cat = jnp.concatenate([hm] + parts[N_HEADS:], axis=1).astype(BF16)
    mix = jnp.dot(cat, wout_ref[...], preferred_element_type=F32)
    x1 = x_ref[...] + g1_ref[0] * mix
    x1_ref[...] = x1
    h2 = _rms(x1, ln_ref[...]) * (1.0 + sc_ref[0]) + sh_ref[0]
    hh = h2.astype(BF16)
    _store_tokens(h2_ref, _pack_bf16_pairs(hh))

    n_exp = br_ref.shape[1]
    hl = (h2 - hh.astype(F32)).astype(BF16)
    rl = jnp.dot(jnp.concatenate([hh, hl], axis=0), wrh_ref[...], preferred_element_type=F32)
    logits = (rl[:tm, :n_exp] + rl[tm:, :n_exp] + rl[:tm, n_exp:2 * n_exp]) + br_ref[...]
    lane = lax.broadcasted_iota(jnp.int32, (tm, n_exp), 1).astype(F32)
    lane_o = lax.broadcasted_iota(jnp.int32, (tm, LANES), 1)
    base = base_scr[...]
    ltri = ltri_ref[...]
    idx_out = jnp.zeros((tm, LANES), F32)
    val_out = jnp.zeros((tm, LANES), F32)
    rank_out = jnp.zeros((tm, LANES), F32)
    top0 = None
    denom = jnp.zeros((tm, 1), F32)
    work = logits
    for k in range(TOP_K):
        mx = jnp.max(work, axis=1, keepdims=True)
        ik = jnp.min(jnp.where(work == mx, lane, float(n_exp)), axis=1, keepdims=True)
        hit = lane == ik
        work = jnp.where(hit, -jnp.inf, work)
        if top0 is None:
            top0 = mx
        ek = jnp.exp(mx - top0)
        denom = denom + ek
        onehot = hit.astype(F32)
        before = jnp.dot(ltri, onehot.astype(BF16), preferred_element_type=F32)
        rk = jnp.sum(jnp.where(hit, base + before, 0.0), axis=1, keepdims=True)
        base = base + jnp.sum(onehot, axis=0, keepdims=True)
        idx_out = jnp.where(lane_o == k, ik, idx_out)
        val_out = jnp.where(lane_o == k, ek, val_out)
        rank_out = jnp.where(lane_o == k, rk, rank_out)
    idx_ref[...] = idx_out.astype(jnp.int32)
    gw_ref[...] = val_out / denom
    rank_ref[...] = rank_out
    base_scr[...] = base
    cnt_ref[...] = base


def _mix(hf, hb, proj, pdft, x2, g1, sc2, sh2, hng, ln2_g, w_out, mc, w_four, wr_cat, b_router,
         base0, seq):
    n, d = x2.shape
    mw = hf.shape[1]
    fw = pdft.shape[3]
    n_exp = b_router.shape[0]
    cg = fw // N_GROUPS
    tm = _tile(seq, 256)
    tps = seq // tm
    ltri = jnp.asarray(np.tril(np.ones((tm, tm), np.float32), -1), BF16)
    row = lambda i: (i, 0)
    cst = lambda i: (0, 0)
    bat = lambda i: (i // tps, 0, 0)
    return pl.pallas_call(
        _mix_kernel,
        grid=(n // tm,),
        in_specs=[pl.BlockSpec((tm, mw), row), pl.BlockSpec((tm, mw), row),
                  pl.BlockSpec((tm, mw), lambda i: (i, 3)),
                  pl.BlockSpec((1, 1, tm, fw), lambda i: (i // tps, 0, i % tps, 0)),
                  pl.BlockSpec((1, 1, tm, fw), lambda i: (i // tps, 1, i % tps, 0)),
                  pl.BlockSpec((tm, d), row),
                  pl.BlockSpec((1, 1, d), bat), pl.BlockSpec((1, 1, d), bat), pl.BlockSpec((1, 1, d), bat),
                  pl.BlockSpec((1, mw), cst), pl.BlockSpec((1, d), cst),
                  pl.BlockSpec((mw + fw, d), cst),
                  pl.BlockSpec((2 * cg, cg), cst),
                  pl.BlockSpec((N_GROUPS, cg, cg), lambda i: (0, 0, 0)),
                  pl.BlockSpec((d, wr_cat.shape[1]), cst),
                  pl.BlockSpec((1, n_exp), cst),
                  pl.BlockSpec((tm, tm), cst),
                  pl.BlockSpec((1, n_exp), cst)],
        out_specs=[pl.BlockSpec((tm, d), row), pl.BlockSpec((tm * (d // 2 // LANES), LANES), row),
                   pl.BlockSpec((tm, LANES), row), pl.BlockSpec((tm, LANES), row),
                   pl.BlockSpec((tm, LANES), row), pl.BlockSpec((1, n_exp), cst)],
        out_shape=[jax.ShapeDtypeStruct((n, d), F32), jax.ShapeDtypeStruct((n * (d // 2 // LANES), LANES), jnp.uint32),
                   jax.ShapeDtypeStruct((n, LANES), jnp.int32), jax.ShapeDtypeStruct((n, LANES), F32),
                   jax.ShapeDtypeStruct((n, LANES), F32), jax.ShapeDtypeStruct((1, n_exp), F32)],
        scratch_shapes=[pltpu.VMEM((1, n_exp), F32)],
        compiler_params=_params("arbitrary"),
        name="mix",
    )(hf, hb, proj, pdft, pdft, x2, g1, sc2, sh2, hng.reshape(1, mw), ln2_g.reshape(1, d),
      w_out, mc, w_four, wr_cat, b_router.reshape(1, n_exp), ltri, base0)


def _rows_copy(src, src_row, dst, dst_row, nrows, sem):
    if nrows > 1:
        src_row = pl.multiple_of(src_row, nrows)
        dst_row = pl.multiple_of(dst_row, nrows)
    return pltpu.make_async_copy(src.at[pl.ds(src_row, nrows), :], dst.at[pl.ds(dst_row, nrows), :], sem)


def _for_slot(slot, body):
    for s in (0, 1):
        pl.when(slot == s)(functools.partial(body, s))


def _dispatch_kernel(dest_ref, *refs, tiles, rpt):
    h_refs, (xs_ref, hbuf, sems) = refs[:len(tiles)], refs[len(tiles):]
    tm = hbuf.shape[1] // rpt
    i = pl.program_id(0)
    n_steps = pl.num_programs(0)

    def drain(s):
        def body(t, carry):
            for _ in range(TOP_K):
                _rows_copy(hbuf.at[s], 0, xs_ref, 0, rpt, sems.at[s]).wait()
            return carry
        lax.fori_loop(0, tm, body, 0)

    def step(s):
        @pl.when(i >= 2)
        def _():
            drain(s)

        first = 0
        for h_ref, cnt in zip(h_refs, tiles):
            @pl.when(jnp.logical_and(i >= first, i < first + cnt))
            def _(h_ref=h_ref):
                hbuf[s] = h_ref[...]
            first += cnt

        @pl.when(i >= first)
        def _():
            hbuf[s] = jnp.zeros(hbuf.shape[1:], hbuf.dtype)

        def issue(t, carry):
            for k in range(TOP_K):
                _rows_copy(hbuf.at[s], t * rpt, xs_ref, dest_ref[t * TOP_K + k] * rpt, rpt,
                           sems.at[s]).start(priority=k % 2)
            return carry

        lax.fori_loop(0, tm, issue, 0)

        @pl.when(i == n_steps - 1)
        def _():
            drain(s)

        @pl.when(jnp.logical_and(i == n_steps - 1, i >= 1))
        def _():
            drain(1 - s)

    _for_slot(i % 2, step)


def _dispatch(h2ps, dest_all, rpt):
    tm = ROW_TM
    assert all(h.shape[1] == LANES and h.shape[0] % (tm * rpt) == 0 for h in h2ps)
    tiles = tuple(h.shape[0] // (tm * rpt) for h in h2ps)
    rows = dest_all.shape[0]
    n_steps = rows // (tm * TOP_K)
    in_specs = [pl.BlockSpec((tm * TOP_K,), lambda i: (i,), memory_space=pltpu.SMEM)]
    first = 0
    for cnt in tiles:
        in_specs.append(pl.BlockSpec((tm * rpt, LANES),
                                     lambda i, first=first, cnt=cnt: (jnp.clip(i - first, 0, cnt - 1), 0)))
        first += cnt
    return pl.pallas_call(
        functools.partial(_dispatch_kernel, tiles=tiles, rpt=rpt),
        grid=(n_steps,),
        in_specs=in_specs,
        out_specs=pl.BlockSpec(memory_space=pl.ANY),
        out_shape=jax.ShapeDtypeStruct((rows * rpt, LANES), jnp.uint32),
        scratch_shapes=[pltpu.VMEM((2, tm * rpt, LANES), jnp.uint32), pltpu.SemaphoreType.DMA((2,))],
        compiler_params=_params("arbitrary"),
        name="dispatch",
    )(dest_all, *h2ps)


def _combine_kernel(dcur_ref, dnext_ref, x1_ref, gw_ref, g2_ref, fg_ref, ys_ref, o_ref, buf, sems):
    tm = x1_ref.shape[0]
    rpt = buf.shape[2] // tm
    i = pl.program_id(0)
    n_steps = pl.num_programs(0)

    def issue(d_ref, s):
        def body(t, carry):
            for k in range(TOP_K):
                _rows_copy(ys_ref, d_ref[t * TOP_K + k] * rpt, buf.at[s, k], t * rpt, rpt,
                           sems.at[s]).start(priority=k % 2)
            return carry
        lax.fori_loop(0, tm, body, 0)

    def step(s):
        @pl.when(i == 0)
        def _():
            issue(dcur_ref, s)

        @pl.when(i + 1 < n_steps)
        def _():
            issue(dnext_ref, 1 - s)

        def drain(t, carry):
            for k in range(TOP_K):
                _rows_copy(ys_ref, 0, buf.at[s, k], 0, rpt, sems.at[s]).wait()
            return carry

        lax.fori_loop(0, tm, drain, 0)
        gw = gw_ref[...]
        y = None
        for k in range(TOP_K):
            yk = gw[:, k:k + 1] * _unpack_bf16_pairs(_load_tokens(buf.at[s, k], tm)).astype(F32)
            y = yk if y is None else y + yk
        x2 = x1_ref[...] + g2_ref[0] * y
        o_ref[...] = _rms(x2, fg_ref[...])

    _for_slot(i % 2, step)


def _combine(x1, gw, g2, final_g, ys, dest, seq):
    n, d = x1.shape
    tm = _tile(seq, ROW_TM)
    tps = seq // tm
    last = n // tm - 1
    return pl.pallas_call(
        _combine_kernel,
        grid=(n // tm,),
        in_specs=[pl.BlockSpec((tm * TOP_K,), lambda i: (i,), memory_space=pltpu.SMEM),
                  pl.BlockSpec((tm * TOP_K,), lambda i: (jnp.minimum(i + 1, last),), memory_space=pltpu.SMEM),
                  pl.BlockSpec((tm, d), lambda i: (i, 0)),
                  pl.BlockSpec((tm, LANES), lambda i: (i, 0)),
                  pl.BlockSpec((1, 1, d), lambda i: (i // tps, 0, 0)),
                  pl.BlockSpec((1, d), lambda i: (0, 0)),
                  pl.BlockSpec(memory_space=pl.ANY)],
        out_specs=pl.BlockSpec((tm, d), lambda i: (i, 0)),
        out_shape=jax.ShapeDtypeStruct((n, d), F32),
        scratch_shapes=[pltpu.VMEM((2, TOP_K, tm * (d // 2 // LANES), LANES), jnp.uint32),
                        pltpu.SemaphoreType.DMA((2,))],
        compiler_params=_params("arbitrary"),
        name="combine",
    )(dest, dest, x1, gw, g2, final_g.reshape(1, d), ys)


def _load_tokens(ref, n_tok, first=0, rpt=None):
    if rpt is None:
        rpt = ref.shape[0] // n_tok
    return jnp.concatenate([ref[pl.ds(first * rpt + s, n_tok, stride=rpt), :] for s in range(rpt)], axis=1)


def _store_tokens(ref, x):
    n_tok = x.shape[0]
    rpt = x.shape[1] // LANES
    for s in range(rpt):
        ref[pl.ds(s, n_tok, stride=rpt), :] = x[:, s * LANES:(s + 1) * LANES]


def _moe_kernel(be_ref, bv_ref, x_ref, wg_ref, wu_ref, wd_ref, bg_ref, bu_ref, bd_ref, o_ref, xb_scr, acc):
    del be_ref
    n_valid = bv_ref[pl.program_id(0)]
    f = pl.program_id(1)
    bm = xb_scr.shape[0]

    @pl.when(f == 0)
    def _():
        acc[...] = jnp.broadcast_to(bd_ref[0], acc.shape)

    @pl.when(jnp.logical_and(n_valid > 0, f == 0))
    def _():
        xb_scr[...] = _unpack_bf16_pairs(_load_tokens(x_ref, bm))

    def experts(rows):
        xb = xb_scr[0:rows, :]
        gate = jnp.dot(xb, wg_ref[0], preferred_element_type=F32) + bg_ref[0]
        up = jnp.dot(xb, wu_ref[0], preferred_element_type=F32) + bu_ref[0]
        gate = jnp.minimum(gate, SWIGLU_LIMIT)
        up = jnp.clip(up, -SWIGLU_LIMIT, SWIGLU_LIMIT)
        act = (up + 1.0) * (gate * jax.nn.sigmoid(SWIGLU_ALPHA * gate))
        acc[0:rows, :] += jnp.dot(act.astype(BF16), wd_ref[0], preferred_element_type=F32)

    pl.when(n_valid > bm // 2)(functools.partial(experts, bm))
    pl.when(jnp.logical_and(n_valid > 0, n_valid <= bm // 2))(functools.partial(experts, bm // 2))

    @pl.when(f == pl.num_programs(1) - 1)
    def _():
        _store_tokens(o_ref, _pack_bf16_pairs(acc[...].astype(BF16)))


def _moe(xs, block_e, block_valid, w_gu, b_gu, w_dn, b_dn):
    n_exp, d_ff, d = w_dn.shape
    rpt = d // 2 // LANES
    rows = xs.shape[0] // rpt
    fc = _tile(d_ff, 1024)
    nf = d_ff // fc
    n_blocks = rows // MOE_BM
    grid_spec = pltpu.PrefetchScalarGridSpec(
        num_scalar_prefetch=2,
        grid=(n_blocks, nf),
        in_specs=[pl.BlockSpec((MOE_BM * rpt, LANES), lambda i, f, be, bv: (i, 0)),
                  pl.BlockSpec((1, d, fc), lambda i, f, be, bv: (be[i], 0, f)),
                  pl.BlockSpec((1, d, fc), lambda i, f, be, bv: (be[i], 0, nf + f)),
                  pl.BlockSpec((1, fc, d), lambda i, f, be, bv: (be[i], f, 0)),
                  pl.BlockSpec((1, 1, fc), lambda i, f, be, bv: (be[i], 0, f)),
                  pl.BlockSpec((1, 1, fc), lambda i, f, be, bv: (be[i], 0, nf + f)),
                  pl.BlockSpec((1, 1, d), lambda i, f, be, bv: (be[i], 0, 0))],
        out_specs=pl.BlockSpec((MOE_BM * rpt, LANES), lambda i, f, be, bv: (i, 0)),
        scratch_shapes=[pltpu.VMEM((MOE_BM, d), BF16), pltpu.VMEM((MOE_BM, d), F32)],
    )
    return pl.pallas_call(
        _moe_kernel,
        grid_spec=grid_spec,
        out_shape=jax.ShapeDtypeStruct((rows * rpt, LANES), jnp.uint32),
        compiler_params=_params("arbitrary", "arbitrary"),
        name="moe",
    )(block_e, block_valid, xs, w_gu, w_gu, w_dn,
      b_gu.reshape(n_exp, 1, 2 * d_ff), b_gu.reshape(n_exp, 1, 2 * d_ff), b_dn.reshape(n_exp, 1, d))


def _mixer_and_router(x, ada, wts, base0):
    (ln1_g, ln2_g, w_main, w_gates, b_gates, conv_w, conv_b, head_norm_g, w_four, w_out_b, mc,
     wr_cat, b_router) = wts
    bsz, seq, d = x.shape
    n = bsz * seq
    mw = d // 2
    dh = mw // N_HEADS
    x2 = x.reshape(n, d)
    sh1, sc1, g1, sh2, sc2, g2 = [a.reshape(bsz, 1, d) for a in jnp.split(ada, 6, axis=-1)]

    proj, gates = _inproj(x2, ln1_g, sc1, sh1, w_main, w_gates, b_gates, seq)
    q, kt = _conv(proj, conv_w, conv_b, seq, mw, dh ** -0.5)
    gates_t = gates[:, :N_GATES].reshape(bsz, seq, N_GATES).transpose(0, 2, 1).reshape(bsz * N_GATES, seq)
    hf, hb = _mlstm(q, kt, proj, gates, gates_t, bsz, seq, mw)
    assert d - mw == mw, "the Fourier block is addressed as column block 4 of proj"
    pdft = _position_dft(proj, bsz, seq, mw, 4)
    x1, h2p, idx, gw, rank, cnt = _mix(hf, hb, proj, pdft, x2, g1, sc2, sh2, head_norm_g, ln2_g,
                                       w_out_b, mc, w_four, wr_cat, b_router, base0, seq)
    return dict(x1=x1, h2p=h2p, idx=idx[:, :TOP_K], gw=gw, rank=rank[:, :TOP_K].astype(jnp.int32),
                cnt=cnt, g2=g2, shape=(bsz, seq, d))


def kernel(x_prompt, x_sample, c_prompt, c_sample, ln1_g, ln2_g, w_ada, b_ada, w_in, b_gates, conv_w,
           conv_b, head_norm_g, w_four, w_out, w_router, b_router, w_gate_up, b_gate_up, w_down, b_down,
           final_g):
    assert w_ada.shape[0] == 1, "single-layer trunk"
    d = x_prompt.shape[-1]
    mw = d // 2
    cg = (d - mw) // N_GROUPS
    w_in0 = w_in[0]
    w_main = jnp.concatenate([w_in0[:, :4 * mw], w_in0[:, 4 * mw + N_GATES:]], axis=1).astype(BF16)
    w_gates = jnp.pad(w_in0[:, 4 * mw:4 * mw + N_GATES], ((0, 0), (0, LANES - N_GATES))).astype(BF16)
    bg = jnp.pad(b_gates[0], (0, LANES - N_GATES)).reshape(1, LANES)
    kc = np.arange(cg)
    angc = 2.0 * np.pi * np.outer(kc, kc) / cg
    mc = jnp.asarray(np.concatenate([np.cos(angc), -np.sin(angc)], axis=0) / math.sqrt(cg), BF16)
    wr = w_router[0]
    wr_hi = wr.astype(BF16)
    wr_lo = (wr - wr_hi.astype(F32)).astype(BF16)
    n_exp = wr.shape[1]
    mxu_cols = 2 * LANES
    wr_cat = jnp.pad(jnp.concatenate([wr_hi, wr_lo], axis=1), ((0, 0), (0, max(0, mxu_cols - 2 * n_exp))))
    wts = (ln1_g[0], ln2_g[0], w_main, w_gates, bg, conv_w[0], conv_b[0], head_norm_g[0], w_four[0],
           w_out[0].astype(BF16), mc, wr_cat, b_router[0])
    nbp = c_prompt.shape[0]
    ada = _ada(jnp.concatenate([c_prompt, c_sample], axis=0), w_ada[0], b_ada[0])
    gp = _mixer_and_router(x_prompt, ada[:nbp], wts, jnp.zeros((1, n_exp), F32))
    gs = _mixer_and_router(x_sample, ada[nbp:], wts, gp["cnt"])
    groups = (gp, gs)

    counts = gs["cnt"][0].astype(jnp.int32)
    p_counts = (counts + MOE_BM - 1) // MOE_BM * MOE_BM
    p_ends = jnp.cumsum(p_counts)
    p_off = p_ends - p_counts
    n_tok = sum(g["x1"].shape[0] for g in groups)
    n_blocks = (n_tok * TOP_K) // MOE_BM + n_exp
    starts = jnp.arange(n_blocks, dtype=jnp.int32) * MOE_BM
    block_e = jnp.minimum(jnp.sum((starts[:, None] >= p_ends[None, :]).astype(jnp.int32), axis=1), n_exp - 1)
    block_valid = jnp.clip((p_off + counts)[block_e] - starts, 0, MOE_BM).astype(jnp.int32)
    dests = [(p_off[g["idx"]] + g["rank"]).reshape(-1) for g in groups]
    rows = n_blocks * MOE_BM
    gap_start = jnp.concatenate([p_off + counts, p_ends[-1:]])
    gap_len = jnp.concatenate([p_counts - counts, rows - p_ends[-1:]])
    gap_end = jnp.cumsum(gap_len)
    j = jnp.arange(n_exp * MOE_BM, dtype=jnp.int32)
    gi = jnp.sum((j[:, None] >= gap_end[None, :]).astype(jnp.int32), axis=1)
    pad_dest = (gap_start[gi] + (j - (gap_end - gap_len)[gi])).astype(jnp.int32)

    xs = _dispatch([g["h2p"] for g in groups], jnp.concatenate(dests + [pad_dest]), d // 2 // LANES)
    ys = _moe(xs, block_e, block_valid, w_gate_up[0].astype(BF16), b_gate_up[0], w_down[0].astype(BF16), b_down[0])
    outs = []
    for g, dest in zip(groups, dests):
        bsz, seq, _ = g["shape"]
        outs.append(_combine(g["x1"], g["gw"], g["g2"], final_g, ys, dest, seq).reshape(g["shape"]))
    return tuple(outs)
```

```python
import functools
import math

import numpy as np
import jax
import jax.numpy as jnp
from jax import lax
from jax.experimental import pallas as pl
from jax.experimental.pallas import tpu as pltpu

F32 = jnp.float32
BF16 = jnp.bfloat16

N_HEADS = 4
N_GROUPS = 4
N_GATES = 4 * N_HEADS
CHUNK = 128
TOP_K = 4
SWIGLU_LIMIT = 7.0
SWIGLU_ALPHA = 1.702
EPS = 1e-6
LANES = 128
SUBLANES = 8
DFT_N2 = 128
MOE_BM = 512
ROW_TM = 256
VMEM_LIMIT = 56 * 1024 * 1024


def _params(*sem, flags=None):
    return pltpu.CompilerParams(dimension_semantics=sem, vmem_limit_bytes=VMEM_LIMIT, flags=flags)


def _tile(n, pref):
    t = min(n, pref)
    while n % t:
        t //= 2
    return t


def _ada_kernel(c_ref, w_ref, b_ref, o_ref):
    c = c_ref[...]
    s = (c * jax.nn.sigmoid(c)).astype(BF16)
    o_ref[...] = jnp.dot(s, w_ref[...].astype(BF16), preferred_element_type=F32) + b_ref[...]


def _ada(c, w_ada, b_ada):
    bsz, d = c.shape
    bp = -(-bsz // SUBLANES) * SUBLANES
    cp = jnp.pad(c, ((0, bp - bsz), (0, 0)))
    n_out = w_ada.shape[1]
    tn = _tile(n_out, 1024)
    out = pl.pallas_call(
        _ada_kernel,
        grid=(n_out // tn,),
        in_specs=[pl.BlockSpec((bp, d), lambda j: (0, 0)),
                  pl.BlockSpec((d, tn), lambda j: (0, j)),
                  pl.BlockSpec((1, tn), lambda j: (0, j))],
        out_specs=pl.BlockSpec((bp, tn), lambda j: (0, j)),
        out_shape=jax.ShapeDtypeStruct((bp, n_out), F32),
        compiler_params=_params("arbitrary"),
        name="ada",
    )(cp, w_ada, b_ada.reshape(1, n_out))
    return out[:bsz]


def _rms(x, g):
    return (x * lax.rsqrt(jnp.mean(x * x, axis=-1, keepdims=True) + EPS)) * g


def _inproj_kernel(x_ref, g_ref, sc_ref, sh_ref, w_ref, wg_ref, bg_ref, o_ref, og_ref, h_scr):
    @pl.when(pl.program_id(1) == 0)
    def _():
        h = _rms(x_ref[...], g_ref[...]) * (1.0 + sc_ref[0]) + sh_ref[0]
        hb = h.astype(BF16)
        h_scr[...] = hb
        og_ref[...] = jnp.dot(hb, wg_ref[...], preferred_element_type=F32) + bg_ref[...]

    o_ref[...] = jnp.dot(h_scr[...], w_ref[...], preferred_element_type=F32)


def _inproj(x2, ln_g, sc, sh, w_main, w_gates, b_gates, seq):
    n, d = x2.shape
    p = w_main.shape[1]
    tm = _tile(seq, 1024)
    tn = _tile(p, 1024)
    tps = seq // tm
    return pl.pallas_call(
        _inproj_kernel,
        grid=(n // tm, p // tn),
        in_specs=[pl.BlockSpec((tm, d), lambda i, j: (i, 0)),
                  pl.BlockSpec((1, d), lambda i, j: (0, 0)),
                  pl.BlockSpec((1, 1, d), lambda i, j: (i // tps, 0, 0)),
                  pl.BlockSpec((1, 1, d), lambda i, j: (i // tps, 0, 0)),
                  pl.BlockSpec((d, tn), lambda i, j: (0, j)),
                  pl.BlockSpec((d, LANES), lambda i, j: (0, 0)),
                  pl.BlockSpec((1, LANES), lambda i, j: (0, 0))],
        out_specs=[pl.BlockSpec((tm, tn), lambda i, j: (i, j)),
                   pl.BlockSpec((tm, LANES), lambda i, j: (i, 0))],
        out_shape=[jax.ShapeDtypeStruct((n, p), F32),
                   jax.ShapeDtypeStruct((n, LANES), F32)],
        scratch_shapes=[pltpu.VMEM((tm, d), BF16)],
        compiler_params=_params("arbitrary", "arbitrary"),
        name="inproj",
    )(x2, ln_g.reshape(1, d), sc, sh, w_main, w_gates, b_gates)


def _conv_kernel(x_ref, prev_ref, next_ref, w_ref, b_ref, q_ref, kt_ref, res_scr, *, tps, k_scale):
    it = pl.program_id(0) % tps
    w = w_ref[...]
    b = b_ref[...]
    taps = w.shape[0]
    half = taps // 2

    def conv_act(z):
        nz = z.shape[0]
        acc = z * w[half:half + 1]
        for j in range(taps):
            if j != half:
                acc = acc + pltpu.roll(z, (half - j) % nz, 0) * w[j:j + 1]
        acc = acc + b
        return acc * jax.nn.sigmoid(acc)

    x = x_ref[...]
    tm = x.shape[0]
    res_scr[...] = conv_act(x)
    prev = jnp.where(it == 0, 0.0, prev_ref[...])
    nxt = jnp.where(it == tps - 1, 0.0, next_ref[...])
    top = conv_act(jnp.concatenate([prev, x[0:2 * SUBLANES]], axis=0))
    res_scr[0:SUBLANES, :] = top[SUBLANES:2 * SUBLANES]
    bot = conv_act(jnp.concatenate([x[tm - 2 * SUBLANES:tm], nxt], axis=0))
    res_scr[tm - SUBLANES:tm, :] = bot[SUBLANES:2 * SUBLANES]

    @pl.when(pl.program_id(1) == 0)
    def _():
        q_ref[...] = res_scr[...]

    @pl.when(pl.program_id(1) == 1)
    def _():
        for r in range(0, tm, LANES):
            kt_ref[:, r:r + LANES] = (res_scr[r:r + LANES, :] * k_scale).T


def _conv(proj, conv_w, conv_b, seq, mw, k_scale):
    n = proj.shape[0]
    taps = conv_w.shape[0]
    assert taps // 2 <= SUBLANES
    tm = _tile(seq, 512)
    assert tm >= 4 * SUBLANES
    tps = seq // tm
    r8 = tm // SUBLANES
    last8 = n // SUBLANES - 1
    return pl.pallas_call(
        functools.partial(_conv_kernel, tps=tps, k_scale=k_scale),
        grid=(n // tm, 2),
        in_specs=[pl.BlockSpec((tm, mw), lambda i, j: (i, j)),
                  pl.BlockSpec((SUBLANES, mw), lambda i, j: (jnp.maximum(i * r8 - 1, 0), j)),
                  pl.BlockSpec((SUBLANES, mw), lambda i, j: (jnp.minimum((i + 1) * r8, last8), j)),
                  pl.BlockSpec((taps, mw), lambda i, j: (0, j)),
                  pl.BlockSpec((1, mw), lambda i, j: (0, j))],
        out_specs=[pl.BlockSpec((tm, mw), lambda i, j: (i, 0)),
                   pl.BlockSpec((mw, tm), lambda i, j: (0, i))],
        out_shape=[jax.ShapeDtypeStruct((n, mw), F32), jax.ShapeDtypeStruct((mw, n), F32)],
        scratch_shapes=[pltpu.VMEM((tm, mw), F32)],
        compiler_params=_params("arbitrary", "arbitrary"),
        name="conv",
    )(proj, proj, proj, conv_w, conv_b.reshape(1, 2 * mw))


def _log_sigmoid(x):
    return jnp.minimum(x, 0.0) - jnp.log1p(jnp.exp(-jnp.abs(x)))


def _split3(x):
    hi = x.astype(BF16)
    r = x - hi.astype(F32)
    mid = r.astype(BF16)
    lo = (r - mid.astype(F32)).astype(BF16)
    return hi, mid, lo


def _cummax(x, axis, reverse):
    n = x.shape[axis]
    pos = lax.broadcasted_iota(jnp.int32, x.shape, axis)
    step = 1
    while step < n:
        if reverse:
            shifted = jnp.where(pos < n - step, pltpu.roll(x, n - step, axis), -jnp.inf)
        else:
            shifted = jnp.where(pos >= step, pltpu.roll(x, step, axis), -jnp.inf)
        x = jnp.maximum(x, shifted)
        step *= 2
    return x


def _mlstm_gates(gc, gr, tri_c, tri_r, m_lane, m_sub, reverse):
    length = gc.shape[0]
    last = 0 if reverse else length - 1
    b_cols = jnp.dot(tri_c, jnp.concatenate(_split3(_log_sigmoid(gc)), axis=0), preferred_element_type=F32)
    b_rows = jnp.dot(jnp.concatenate(_split3(_log_sigmoid(gr)), axis=1), tri_r, preferred_element_type=F32)
    b_cols = pltpu.roll(b_cols, LANES - N_HEADS, 1)
    b_rows = pltpu.roll(b_rows, N_GATES - N_HEADS, 0)
    u_cols = gc - b_cols
    u_rows = gr - b_rows
    c_cols = jnp.maximum(m_lane, _cummax(u_cols, 0, reverse))
    c_rows = jnp.maximum(m_sub, _cummax(u_rows, 1, reverse))
    c_last = c_rows[:, last:last + 1]
    out = dict(
        c_cols=c_cols, u_rows=u_rows,
        ea_cols=jnp.exp(m_lane - c_cols),
        floor_cols=jnp.exp(-b_cols - c_cols),
        ew_rows=jnp.exp(u_rows - c_last),
        decay_rows=jnp.exp(m_sub - c_last),
        m_lane=b_cols[last:last + 1, :] + c_cols[last:last + 1, :],
        m_sub=jnp.broadcast_to(b_rows[:, last:last + 1] + c_last, m_sub.shape),
    )
    return out


def _mlstm_heads(work, gate_fn, cn_ref):
    length = work[0][0].shape[0]
    row = lax.broadcasted_iota(jnp.int32, (length, length), 0)
    col = lax.broadcasted_iota(jnp.int32, (length, length), 1)
    qxs = []
    for q_ref, kt_ref, v_ref, h_ref, sl, di, ci, idx, reverse in work:
        rhs = jnp.concatenate([kt_ref[sl, :].astype(BF16), cn_ref[idx].astype(BF16)], axis=1)
        qxs.append(jnp.dot(q_ref[:, sl].astype(BF16), rhs, preferred_element_type=F32))
    gts = gate_fn()
    work = [w[:5] + (gts[w[5]],) + w[6:] for w in work]
    wms, lhs, wvs = [], [], []
    for (q_ref, kt_ref, v_ref, h_ref, sl, gt, ci, idx, reverse), qx in zip(work, qxs):
        mask = (col >= row) if reverse else (col <= row)
        wm = jnp.exp(jnp.where(mask, gt["u_rows"][ci:ci + 1, :] - gt["c_cols"][:, ci:ci + 1], -jnp.inf))
        wm = wm * qx[:, :length]
        kw = kt_ref[sl, :] * gt["ew_rows"][ci:ci + 1, :]
        wms.append(wm)
        lhs.append(jnp.concatenate([wm, kw], axis=0).astype(BF16))
    for (q_ref, kt_ref, v_ref, h_ref, sl, gt, ci, idx, reverse), lh in zip(work, lhs):
        v1 = jnp.concatenate([v_ref[:, sl].astype(BF16), jnp.ones((length, LANES), BF16)], axis=1)
        wvs.append(jnp.dot(lh, v1, preferred_element_type=F32))
    for (q_ref, kt_ref, v_ref, h_ref, sl, gt, ci, idx, reverse), qx, wm, wv in zip(work, qxs, wms, wvs):
        dh = wv.shape[1] - LANES
        ea = gt["ea_cols"][:, ci:ci + 1]
        num = wv[:length, :dh] + ea * qx[:, length:length + dh]
        den = jnp.sum(wm, axis=1, keepdims=True) + ea * qx[:, length + dh:length + dh + 1]
        h_ref[:, sl] = num / jnp.maximum(jnp.abs(den), gt["floor_cols"][:, ci:ci + 1])
        width = wv.shape[1]
        decay = jnp.concatenate([gt["decay_rows"][ci:ci + 1, :]] * pl.cdiv(width, LANES), axis=1)[:, :width]
        cn_ref[idx] = decay * cn_ref[idx] + wv[length:, :]


def _mlstm_kernel(qf_ref, kf_ref, vf_ref, gcf_ref, grf_ref,
                  qb_ref, kb_ref, vb_ref, gcb_ref, grb_ref,
                  l3_ref, u3_ref, l3t_ref, u3t_ref,
                  hf_ref, hb_ref, cn_scr, ml_scr, ms_scr):
    @pl.when(pl.program_id(1) == 0)
    def _():
        cn_scr[...] = jnp.zeros_like(cn_scr)
        ml_scr[...] = jnp.zeros_like(ml_scr)
        ms_scr[...] = jnp.zeros_like(ms_scr)

    dh = qf_ref.shape[1] // N_HEADS
    dirs = (
        (False, qf_ref, kf_ref, vf_ref, gcf_ref, grf_ref, l3_ref, u3t_ref, hf_ref),
        (True, qb_ref, kb_ref, vb_ref, gcb_ref, grb_ref, u3_ref, l3t_ref, hb_ref),
    )
    work = []
    for di, (reverse, q_ref, kt_ref, v_ref, gc_ref, gr_ref, tri_c, tri_r, h_ref) in enumerate(dirs):
        for hd in range(N_HEADS):
            sl = slice(hd * dh, (hd + 1) * dh)
            work.append((q_ref, kt_ref, v_ref, h_ref, sl, di, di * 2 * N_HEADS + hd, di * N_HEADS + hd, reverse))

    def gate_fn():
        gts = []
        for di, (reverse, q_ref, kt_ref, v_ref, gc_ref, gr_ref, tri_c, tri_r, h_ref) in enumerate(dirs):
            gt = _mlstm_gates(gc_ref[...], gr_ref[...], tri_c[...], tri_r[...], ml_scr[di], ms_scr[di], reverse)
            ml_scr[di] = gt["m_lane"]
            ms_scr[di] = gt["m_sub"]
            gts.append(gt)
        return gts

    _mlstm_heads(work, gate_fn, cn_scr)


def _tri_consts(length):
    lower = np.tril(np.ones((length, length), np.float32))
    upper = lower.T
    l3 = np.concatenate([lower] * 3, axis=1)
    u3 = np.concatenate([upper] * 3, axis=1)
    l3t = np.concatenate([lower] * 3, axis=0)
    u3t = np.concatenate([upper] * 3, axis=0)
    return tuple(jnp.asarray(a, BF16) for a in (l3, u3, l3t, u3t))


def _mlstm(q, kt, proj, gates, gates_t, bsz, seq, mw):
    n = q.shape[0]
    nc = seq // CHUNK
    dh = mw // N_HEADS
    assert CHUNK == LANES
    fwd = lambda b, c: b * nc + c
    bwd = lambda b, c: b * nc + (nc - 1 - c)
    cst = lambda b, c: (0, 0)
    in_specs = []
    for pos in (fwd, bwd):
        in_specs += [
            pl.BlockSpec((CHUNK, mw), lambda b, c, pos=pos: (pos(b, c), 0)),
            pl.BlockSpec((mw, CHUNK), lambda b, c, pos=pos: (0, pos(b, c))),
            pl.BlockSpec((CHUNK, mw), lambda b, c, pos=pos: (pos(b, c), 2)),
            pl.BlockSpec((CHUNK, LANES), lambda b, c, pos=pos: (pos(b, c), 0)),
            pl.BlockSpec((N_GATES, CHUNK), lambda b, c, pos=pos: (b, pos(0, c))),
        ]
    in_specs += [pl.BlockSpec((CHUNK, 3 * CHUNK), cst), pl.BlockSpec((CHUNK, 3 * CHUNK), cst),
                 pl.BlockSpec((3 * CHUNK, CHUNK), cst), pl.BlockSpec((3 * CHUNK, CHUNK), cst)]
    return pl.pallas_call(
        _mlstm_kernel,
        grid=(bsz, nc),
        in_specs=in_specs,
        out_specs=[pl.BlockSpec((CHUNK, mw), lambda b, c: (fwd(b, c), 0)),
                   pl.BlockSpec((CHUNK, mw), lambda b, c: (bwd(b, c), 0))],
        out_shape=[jax.ShapeDtypeStruct((n, mw), F32), jax.ShapeDtypeStruct((n, mw), F32)],
        scratch_shapes=[pltpu.VMEM((2 * N_HEADS, dh, dh + LANES), F32),
                        pltpu.VMEM((2, 1, LANES), F32),
                        pltpu.VMEM((2, N_GATES, LANES), F32)],
        compiler_params=_params("arbitrary", "arbitrary"),
        name="mlstm",
    )(q, kt, proj, gates, gates_t, q, kt, proj, gates, gates_t, *_tri_consts(CHUNK))


def _dft_a_kernel(x_ref, ga_ref, tc_ref, ts_ref, o_ref, *, jb):
    n1, width = x_ref.shape[1], x_ref.shape[3]
    rows = n1 * SUBLANES
    reps = width // LANES
    ga = ga_ref[...]
    for s2 in range(jb // 2):
        cos_parts, sin_parts = [], []
        for s in (2 * s2, 2 * s2 + 1):
            sl = slice(s * SUBLANES, (s + 1) * SUBLANES)
            x = x_ref[0, :, sl, :].reshape(rows, width).astype(BF16)
            z = jnp.dot(ga, x, preferred_element_type=F32)
            zc, zs = z[:rows], z[rows:]
            tc = jnp.concatenate([tc_ref[s]] * reps, axis=1)
            ts = jnp.concatenate([ts_ref[s]] * reps, axis=1)
            cos_parts.append((zc * tc - zs * ts).reshape(n1, SUBLANES, width))
            sin_parts.append((zc * ts + zs * tc).reshape(n1, SUBLANES, width))
        sl2 = slice(2 * s2 * SUBLANES, (2 * s2 + 2) * SUBLANES)
        o_ref[0, 0, :, sl2, :] = jnp.concatenate(cos_parts, axis=1).astype(BF16)
        o_ref[0, 1, :, sl2, :] = jnp.concatenate(sin_parts, axis=1).astype(BF16)


def _dft_b_kernel(z_ref, gb_ref, o_ref, scr):
    n2, wb = z_ref.shape[3], z_ref.shape[4]
    gb = gb_ref[...]
    for k in range(SUBLANES):
        slab = z_ref[0, :, k].reshape(2 * n2, wb)
        p = jnp.dot(gb, slab, preferred_element_type=F32)
        for cs in range(2):
            for c in range(wb // LANES):
                scr[c, pl.ds(cs * n2 * SUBLANES + k, n2, stride=SUBLANES), :] = (
                    p[cs * n2:(cs + 1) * n2, c * LANES:(c + 1) * LANES])
    for c in range(wb // LANES):
        o_ref[0, :, :, :, c * LANES:(c + 1) * LANES] = scr[c].reshape(2, n2, SUBLANES, LANES)


def _position_dft(proj, bsz, seq, width, col_block):
    n2 = DFT_N2
    n1 = seq // n2
    assert seq % n2 == 0 and n1 % SUBLANES == 0 and proj.shape[1] % width == 0
    rows = n1 * SUBLANES
    k1 = np.arange(n1)
    ang1 = 2.0 * np.pi * np.outer(k1, k1) / n1
    eye = np.eye(SUBLANES)
    ga = jnp.asarray(np.concatenate([np.kron(np.cos(ang1), eye), np.kron(np.sin(ang1), eye)], axis=0)
                     / math.sqrt(n1), BF16)
    k2 = np.arange(n2)
    ang2 = 2.0 * np.pi * np.outer(k2, k2) / n2
    c2, s2 = np.cos(ang2), np.sin(ang2)
    gb = jnp.asarray(np.block([[c2, -s2], [s2, c2]]) / math.sqrt(n2), BF16)
    n2_idx = jnp.arange(n2, dtype=jnp.int32).reshape(n2 // SUBLANES, 1, SUBLANES)
    k1_idx = jnp.arange(n1, dtype=jnp.int32).reshape(1, n1, 1)
    ang = ((2.0 * math.pi / seq) * ((n2_idx * k1_idx) % seq).astype(F32)).reshape(n2 // SUBLANES, rows)
    tc = jnp.broadcast_to(jnp.cos(ang)[:, :, None], (n2 // SUBLANES, rows, LANES))
    ts = jnp.broadcast_to(jnp.sin(ang)[:, :, None], (n2 // SUBLANES, rows, LANES))

    jb = max(2, min(n2 // SUBLANES, 512 // rows))
    wa = _tile(width, 512)
    wpa = width // wa
    proj4 = proj.reshape(bsz, n1, n2, proj.shape[1])
    z = pl.pallas_call(
        functools.partial(_dft_a_kernel, jb=jb),
        grid=(bsz, n2 // (SUBLANES * jb), wpa),
        in_specs=[pl.BlockSpec((1, n1, SUBLANES * jb, wa), lambda b, j, w: (b, 0, j, col_block * wpa + w)),
                  pl.BlockSpec((2 * rows, rows), lambda b, j, w: (0, 0)),
                  pl.BlockSpec((jb, rows, LANES), lambda b, j, w: (j, 0, 0)),
                  pl.BlockSpec((jb, rows, LANES), lambda b, j, w: (j, 0, 0))],
        out_specs=pl.BlockSpec((1, 2, n1, SUBLANES * jb, wa), lambda b, j, w: (b, 0, 0, j, w)),
        out_shape=jax.ShapeDtypeStruct((bsz, 2, n1, n2, width), BF16),
        compiler_params=_params("arbitrary", "arbitrary", "arbitrary"),
        name="dft_a",
    )(proj4, ga, tc, ts)
    wb = _tile(width, 512)
    p = pl.pallas_call(
        _dft_b_kernel,
        grid=(bsz, n1 // SUBLANES, width // wb),
        in_specs=[pl.BlockSpec((1, 2, SUBLANES, n2, wb), lambda b, j, w: (b, 0, j, 0, w)),
                  pl.BlockSpec((2 * n2, 2 * n2), lambda b, j, w: (0, 0))],
        out_specs=pl.BlockSpec((1, 2, n2, SUBLANES, wb), lambda b, j, w: (b, 0, 0, j, w)),
        out_shape=jax.ShapeDtypeStruct((bsz, 2, n2, n1, width), F32),
        scratch_shapes=[pltpu.VMEM((wb // LANES, 2 * n2 * SUBLANES, LANES), F32)],
        compiler_params=_params("arbitrary", "arbitrary", "arbitrary"),
        name="dft_b",
    )(z, gb)
    return p.reshape(bsz, 2, seq, width)


def _pack_bf16_pairs(xb):
    c = xb.shape[1] // 2
    lo = pltpu.bitcast(xb[:, :c].astype(F32), jnp.uint32)
    hi = pltpu.bitcast(xb[:, c:].astype(F32), jnp.uint32)
    return (lo >> 16) | hi


def _unpack_bf16_pairs(w):
    lo = pltpu.bitcast(w << 16, F32).astype(BF16)
    hi = pltpu.bitcast(w & jnp.uint32(0xFFFF0000), F32).astype(BF16)
    return jnp.concatenate([lo, hi], axis=1)


def _mix_kernel(hf_ref, hb_ref, op_ref, pc_ref, ps_ref, x_ref, g1_ref, sc_ref, sh_ref,
                hng_ref, ln_ref, wout_ref, mc_ref, wf_ref, wrh_ref, br_ref, ltri_ref, base0_ref,
                x1_ref, h2_ref, idx_ref, gw_ref, rank_ref, cnt_ref, base_scr):
    @pl.when(pl.program_id(0) == 0)
    def _():
        base_scr[...] = base0_ref[...]

    tm, mw = hf_ref.shape
    dh = mw // N_HEADS
    hm = hf_ref[...] + hb_ref[...]
    parts = []
    for hd in range(N_HEADS):
        seg = hm[:, hd * dh:(hd + 1) * dh]
        parts.append(seg * lax.rsqrt(jnp.mean(seg * seg, axis=-1, keepdims=True) + EPS))
    hm = jnp.concatenate(parts, axis=1) * hng_ref[...] * jax.nn.sigmoid(op_ref[...])
    cg = pc_ref.shape[3] // N_GROUPS
    pc = pc_ref[0, 0]
    ps = ps_ref[0, 0]
    specs = []
    for gi in range(N_GROUPS):
        sl = slice(gi * cg, (gi + 1) * cg)
        specs.append(jnp.dot(jnp.concatenate([pc[:, sl], ps[:, sl]], axis=1).astype(BF16), mc_ref[...],
                             preferred_element_type=F32))
    for gi in range(N_GROUPS):
        parts.append(jnp.dot(specs[gi].astype(BF16), wf_ref[gi].astype(BF16), preferred_element_type=F32))
    cat = jnp.concatenate([hm] + parts[N_HEADS:], axis=1).astype(BF16)
    mix = jnp.dot(cat, wout_ref[...], preferred_element_type=F32)
    x1 = x_ref[...] + g1_ref[0] * mix
    x1_ref[...] = x1
    h2 = _rms(x1, ln_ref[...]) * (1.0 + sc_ref[0]) + sh_ref[0]
    hh = h2.astype(BF16)
    _store_tokens(h2_ref, _pack_bf16_pairs(hh))

    n_exp = br_ref.shape[1]
    hl = (h2 - hh.astype(F32)).astype(BF16)
    rl = jnp.dot(jnp.concatenate([hh, hl], axis=0), wrh_ref[...], preferred_element_type=F32)
    logits = (rl[:tm, :n_exp] + rl[tm:, :n_exp] + rl[:tm, n_exp:2 * n_exp]) + br_ref[...]
    lane = lax.broadcasted_iota(jnp.int32, (tm, n_exp), 1).astype(F32)
    lane_o = lax.broadcasted_iota(jnp.int32, (tm, LANES), 1)
    base = base_scr[...]
    ltri = ltri_ref[...]
    idx_out = jnp.zeros((tm, LANES), F32)
    val_out = jnp.zeros((tm, LANES), F32)
    rank_out = jnp.zeros((tm, LANES), F32)
    top0 = None
    denom = jnp.zeros((tm, 1), F32)
    work = logits
    for k in range(TOP_K):
        mx = jnp.max(work, axis=1, keepdims=True)
        ik = jnp.min(jnp.where(work == mx, lane, float(n_exp)), axis=1, keepdims=True)
        hit = lane == ik
        work = jnp.where(hit, -jnp.inf, work)
        if top0 is None:
            top0 = mx
        ek = jnp.exp(mx - top0)
        denom = denom + ek
        onehot = hit.astype(F32)
        before = jnp.dot(ltri, onehot.astype(BF16), preferred_element_type=F32)
        rk = jnp.sum(jnp.where(hit, base + before, 0.0), axis=1, keepdims=True)
        base = base + jnp.sum(onehot, axis=0, keepdims=True)
        idx_out = jnp.where(lane_o == k, ik, idx_out)
        val_out = jnp.where(lane_o == k, ek, val_out)
        rank_out = jnp.where(lane_o == k, rk, rank_out)
    idx_ref[...] = idx_out.astype(jnp.int32)
    gw_ref[...] = val_out / denom
    rank_ref[...] = rank_out
    base_scr[...] = base
    cnt_ref[...] = base


def _mix(hf, hb, proj, pdft, x2, g1, sc2, sh2, hng, ln2_g, w_out, mc, w_four, wr_cat, b_router,
         base0, seq):
    n, d = x2.shape
    mw = hf.shape[1]
    fw = pdft.shape[3]
    n_exp = b_router.shape[0]
    cg = fw // N_GROUPS
    tm = _tile(seq, 256)
    tps = seq // tm
    ltri = jnp.asarray(np.tril(np.ones((tm, tm), np.float32), -1), BF16)
    row = lambda i: (i, 0)
    cst = lambda i: (0, 0)
    bat = lambda i: (i // tps, 0, 0)
    return pl.pallas_call(
        _mix_kernel,
        grid=(n // tm,),
        in_specs=[pl.BlockSpec((tm, mw), row), pl.BlockSpec((tm, mw), row),
                  pl.BlockSpec((tm, mw), lambda i: (i, 3)),
                  pl.BlockSpec((1, 1, tm, fw), lambda i: (i // tps, 0, i % tps, 0)),
                  pl.BlockSpec((1, 1, tm, fw), lambda i: (i // tps, 1, i % tps, 0)),
                  pl.BlockSpec((tm, d), row),
                  pl.BlockSpec((1, 1, d), bat), pl.BlockSpec((1, 1, d), bat), pl.BlockSpec((1, 1, d), bat),
                  pl.BlockSpec((1, mw), cst), pl.BlockSpec((1, d), cst),
                  pl.BlockSpec((mw + fw, d), cst),
                  pl.BlockSpec((2 * cg, cg), cst),
                  pl.BlockSpec((N_GROUPS, cg, cg), lambda i: (0, 0, 0)),
                  pl.BlockSpec((d, wr_cat.shape[1]), cst),
                  pl.BlockSpec((1, n_exp), cst),
                  pl.BlockSpec((tm, tm), cst),
                  pl.BlockSpec((1, n_exp), cst)],
        out_specs=[pl.BlockSpec((tm, d), row), pl.BlockSpec((tm * (d // 2 // LANES), LANES), row),
                   pl.BlockSpec((tm, LANES), row), pl.BlockSpec((tm, LANES), row),
                   pl.BlockSpec((tm, LANES), row), pl.BlockSpec((1, n_exp), cst)],
        out_shape=[jax.ShapeDtypeStruct((n, d), F32), jax.ShapeDtypeStruct((n * (d // 2 // LANES), LANES), jnp.uint32),
                   jax.ShapeDtypeStruct((n, LANES), jnp.int32), jax.ShapeDtypeStruct((n, LANES), F32),
                   jax.ShapeDtypeStruct((n, LANES), F32), jax.ShapeDtypeStruct((1, n_exp), F32)],
        scratch_shapes=[pltpu.VMEM((1, n_exp), F32)],
        compiler_params=_params("arbitrary"),
        name="mix",
    )(hf, hb, proj, pdft, pdft, x2, g1, sc2, sh2, hng.reshape(1, mw), ln2_g.reshape(1, d),
      w_out, mc, w_four, wr_cat, b_router.reshape(1, n_exp), ltri, base0)


def _rows_copy(src, src_row, dst, dst_row, nrows, sem):
    if nrows > 1:
        src_row = pl.multiple_of(src_row, nrows)
        dst_row = pl.multiple_of(dst_row, nrows)
    return pltpu.make_async_copy(src.at[pl.ds(src_row, nrows), :], dst.at[pl.ds(dst_row, nrows), :], sem)


def _for_slot(slot, body):
    for s in (0, 1):
        pl.when(slot == s)(functools.partial(body, s))


def _dispatch_kernel(dest_ref, *refs, tiles, rpt):
    h_refs, (xs_ref, hbuf, sems) = refs[:len(tiles)], refs[len(tiles):]
    tm = hbuf.shape[1] // rpt
    i = pl.program_id(0)
    n_steps = pl.num_programs(0)

    def drain(s):
        def body(t, carry):
            for _ in range(TOP_K):
                _rows_copy(hbuf.at[s], 0, xs_ref, 0, rpt, sems.at[s]).wait()
            return carry
        lax.fori_loop(0, tm, body, 0)

    def step(s):
        @pl.when(i >= 2)
        def _():
            drain(s)

        first = 0
        for h_ref, cnt in zip(h_refs, tiles):
            @pl.when(jnp.logical_and(i >= first, i < first + cnt))
            def _(h_ref=h_ref):
                hbuf[s] = h_ref[...]
            first += cnt

        @pl.when(i >= first)
        def _():
            hbuf[s] = jnp.zeros(hbuf.shape[1:], hbuf.dtype)

        def issue(t, carry):
            for k in range(TOP_K):
                _rows_copy(hbuf.at[s], t * rpt, xs_ref, dest_ref[t * TOP_K + k] * rpt, rpt,
                           sems.at[s]).start(priority=k % 2)
            return carry

        lax.fori_loop(0, tm, issue, 0)

        @pl.when(i == n_steps - 1)
        def _():
            drain(s)

        @pl.when(jnp.logical_and(i == n_steps - 1, i >= 1))
        def _():
            drain(1 - s)

    _for_slot(i % 2, step)


def _dispatch(h2ps, dest_all, rpt):
    tm = ROW_TM
    assert all(h.shape[1] == LANES and h.shape[0] % (tm * rpt) == 0 for h in h2ps)
    tiles = tuple(h.shape[0] // (tm * rpt) for h in h2ps)
    rows = dest_all.shape[0]
    n_steps = rows // (tm * TOP_K)
    in_specs = [pl.BlockSpec((tm * TOP_K,), lambda i: (i,), memory_space=pltpu.SMEM)]
    first = 0
    for cnt in tiles:
        in_specs.append(pl.BlockSpec((tm * rpt, LANES),
                                     lambda i, first=first, cnt=cnt: (jnp.clip(i - first, 0, cnt - 1), 0)))
        first += cnt
    return pl.pallas_call(
        functools.partial(_dispatch_kernel, tiles=tiles, rpt=rpt),
        grid=(n_steps,),
        in_specs=in_specs,
        out_specs=pl.BlockSpec(memory_space=pl.ANY),
        out_shape=jax.ShapeDtypeStruct((rows * rpt, LANES), jnp.uint32),
        scratch_shapes=[pltpu.VMEM((2, tm * rpt, LANES), jnp.uint32), pltpu.SemaphoreType.DMA((2,))],
        compiler_params=_params("arbitrary"),
        name="dispatch",
    )(dest_all, *h2ps)


def _combine_kernel(dcur_ref, dnext_ref, x1_ref, gw_ref, g2_ref, fg_ref, ys_ref, o_ref, buf, sems):
    tm = x1_ref.shape[0]
    rpt = buf.shape[2] // tm
    i = pl.program_id(0)
    n_steps = pl.num_programs(0)

    def issue(d_ref, s):
        def body(t, carry):
            for k in range(TOP_K):
                _rows_copy(ys_ref, d_ref[t * TOP_K + k] * rpt, buf.at[s, k], t * rpt, rpt,
                           sems.at[s]).start(priority=k % 2)
            return carry
        lax.fori_loop(0, tm, body, 0)

    def step(s):
        @pl.when(i == 0)
        def _():
            issue(dcur_ref, s)

        @pl.when(i + 1 < n_steps)
        def _():
            issue(dnext_ref, 1 - s)

        def drain(t, carry):
            for k in range(TOP_K):
                _rows_copy(ys_ref, 0, buf.at[s, k], 0, rpt, sems.at[s]).wait()
            return carry

        lax.fori_loop(0, tm, drain, 0)
        gw = gw_ref[...]
        y = None
        for k in range(TOP_K):
            yk = gw[:, k:k + 1] * _unpack_bf16_pairs(_load_tokens(buf.at[s, k], tm)).astype(F32)
            y = yk if y is None else y + yk
        x2 = x1_ref[...] + g2_ref[0] * y
        o_ref[...] = _rms(x2, fg_ref[...])

    _for_slot(i % 2, step)


def _combine(x1, gw, g2, final_g, ys, dest, seq):
    n, d = x1.shape
    tm = _tile(seq, ROW_TM)
    tps = seq // tm
    last = n // tm - 1
    return pl.pallas_call(
        _combine_kernel,
        grid=(n // tm,),
        in_specs=[pl.BlockSpec((tm * TOP_K,), lambda i: (i,), memory_space=pltpu.SMEM),
                  pl.BlockSpec((tm * TOP_K,), lambda i: (jnp.minimum(i + 1, last),), memory_space=pltpu.SMEM),
                  pl.BlockSpec((tm, d), lambda i: (i, 0)),
                  pl.BlockSpec((tm, LANES), lambda i: (i, 0)),
                  pl.BlockSpec((1, 1, d), lambda i: (i // tps, 0, 0)),
                  pl.BlockSpec((1, d), lambda i: (0, 0)),
                  pl.BlockSpec(memory_space=pl.ANY)],
        out_specs=pl.BlockSpec((tm, d), lambda i: (i, 0)),
        out_shape=jax.ShapeDtypeStruct((n, d), F32),
        scratch_shapes=[pltpu.VMEM((2, TOP_K, tm * (d // 2 // LANES), LANES), jnp.uint32),
                        pltpu.SemaphoreType.DMA((2,))],
        compiler_params=_params("arbitrary"),
        name="combine",
    )(dest, dest, x1, gw, g2, final_g.reshape(1, d), ys)


def _load_tokens(ref, n_tok, first=0, rpt=None):
    if rpt is None:
        rpt = ref.shape[0] // n_tok
    return jnp.concatenate([ref[pl.ds(first * rpt + s, n_tok, stride=rpt), :] for s in range(rpt)], axis=1)


def _store_tokens(ref, x):
    n_tok = x.shape[0]
    rpt = x.shape[1] // LANES
    for s in range(rpt):
        ref[pl.ds(s, n_tok, stride=rpt), :] = x[:, s * LANES:(s + 1) * LANES]


def _moe_kernel(be_ref, bv_ref, x_ref, wg_ref, wu_ref, wd_ref, bg_ref, bu_ref, bd_ref, o_ref, xb_scr, acc, *, nf):
    del be_ref
    n_valid = bv_ref[pl.program_id(0)]
    f = pl.program_id(1)
    bm = xb_scr.shape[0]
    rpt = x_ref.shape[0] // bm

    def experts(rows, first, last):
        if first:
            xb = _unpack_bf16_pairs(_load_tokens(x_ref, rows, 0, rpt))
            if not last:
                xb_scr[0:rows, :] = xb
        else:
            xb = xb_scr[0:rows, :]
        gate = jnp.dot(xb, wg_ref[0], preferred_element_type=F32) + bg_ref[0]
        up = jnp.dot(xb, wu_ref[0], preferred_element_type=F32) + bu_ref[0]
        gate = jnp.minimum(gate, SWIGLU_LIMIT)
        up = jnp.clip(up, -SWIGLU_LIMIT, SWIGLU_LIMIT)
        act = (up + 1.0) * (gate * jax.nn.sigmoid(SWIGLU_ALPHA * gate))
        y = jnp.dot(act.astype(BF16), wd_ref[0], preferred_element_type=F32)
        y = y + (bd_ref[0] if first else acc[0:rows, :])
        if last:
            _store_tokens(o_ref, _pack_bf16_pairs(y.astype(BF16)))
            if rows < bm:
                o_ref[rows * rpt:bm * rpt, :] = jnp.zeros(((bm - rows) * rpt, LANES), o_ref.dtype)
        else:
            acc[0:rows, :] = y

    for first, last in sorted({(ff == 0, ff == nf - 1) for ff in range(nf)}):
        step = jnp.logical_and((f == 0) if first else (f > 0), (f == nf - 1) if last else (f < nf - 1))
        pl.when(jnp.logical_and(step, n_valid > bm // 2))(functools.partial(experts, bm, first, last))
        pl.when(jnp.logical_and(step, jnp.logical_and(n_valid > 0, n_valid <= bm // 2)))(
            functools.partial(experts, bm // 2, first, last))

    @pl.when(jnp.logical_and(f == nf - 1, n_valid == 0))
    def _():
        o_ref[...] = jnp.zeros_like(o_ref)


def _moe(xs, block_e, block_valid, w_gu, b_gu, w_dn, b_dn):
    n_exp, d_ff, d = w_dn.shape
    rpt = d // 2 // LANES
    rows = xs.shape[0] // rpt
    fc = _tile(d_ff, 1024)
    nf = d_ff // fc
    n_blocks = rows // MOE_BM
    grid_spec = pltpu.PrefetchScalarGridSpec(
        num_scalar_prefetch=2,
        grid=(n_blocks, nf),
        in_specs=[pl.BlockSpec((MOE_BM * rpt, LANES), lambda i, f, be, bv: (i, 0)),
                  pl.BlockSpec((1, d, fc), lambda i, f, be, bv: (be[i], 0, f)),
                  pl.BlockSpec((1, d, fc), lambda i, f, be, bv: (be[i], 0, nf + f)),
                  pl.BlockSpec((1, fc, d), lambda i, f, be, bv: (be[i], f, 0)),
                  pl.BlockSpec((1, 1, fc), lambda i, f, be, bv: (be[i], 0, f)),
                  pl.BlockSpec((1, 1, fc), lambda i, f, be, bv: (be[i], 0, nf + f)),
                  pl.BlockSpec((1, 1, d), lambda i, f, be, bv: (be[i], 0, 0))],
        out_specs=pl.BlockSpec((MOE_BM * rpt, LANES), lambda i, f, be, bv: (i, 0)),
        scratch_shapes=[pltpu.VMEM((MOE_BM, d), BF16), pltpu.VMEM((MOE_BM, d), F32)],
    )
    return pl.pallas_call(
        functools.partial(_moe_kernel, nf=nf),
        grid_spec=grid_spec,
        out_shape=jax.ShapeDtypeStruct((rows * rpt, LANES), jnp.uint32),
        compiler_params=_params("arbitrary", "arbitrary"),
        name="moe",
    )(block_e, block_valid, xs, w_gu, w_gu, w_dn,
      b_gu.reshape(n_exp, 1, 2 * d_ff), b_gu.reshape(n_exp, 1, 2 * d_ff), b_dn.reshape(n_exp, 1, d))


def _mixer_and_router(x, ada, wts, base0):
    (ln1_g, ln2_g, w_main, w_gates, b_gates, conv_w, conv_b, head_norm_g, w_four, w_out_b, mc,
     wr_cat, b_router) = wts
    bsz, seq, d = x.shape
    n = bsz * seq
    mw = d // 2
    dh = mw // N_HEADS
    x2 = x.reshape(n, d)
    sh1, sc1, g1, sh2, sc2, g2 = [a.reshape(bsz, 1, d) for a in jnp.split(ada, 6, axis=-1)]

    proj, gates = _inproj(x2, ln1_g, sc1, sh1, w_main, w_gates, b_gates, seq)
    q, kt = _conv(proj, conv_w, conv_b, seq, mw, dh ** -0.5)
    gates_t = gates[:, :N_GATES].reshape(bsz, seq, N_GATES).transpose(0, 2, 1).reshape(bsz * N_GATES, seq)
    hf, hb = _mlstm(q, kt, proj, gates, gates_t, bsz, seq, mw)
    assert d - mw == mw, "the Fourier block is addressed as column block 4 of proj"
    pdft = _position_dft(proj, bsz, seq, mw, 4)
    x1, h2p, idx, gw, rank, cnt = _mix(hf, hb, proj, pdft, x2, g1, sc2, sh2, head_norm_g, ln2_g,
                                       w_out_b, mc, w_four, wr_cat, b_router, base0, seq)
    return dict(x1=x1, h2p=h2p, idx=idx[:, :TOP_K], gw=gw, rank=rank[:, :TOP_K].astype(jnp.int32),
                cnt=cnt, g2=g2, shape=(bsz, seq, d))


def kernel(x_prompt, x_sample, c_prompt, c_sample, ln1_g, ln2_g, w_ada, b_ada, w_in, b_gates, conv_w,
           conv_b, head_norm_g, w_four, w_out, w_router, b_router, w_gate_up, b_gate_up, w_down, b_down,
           final_g):
    assert w_ada.shape[0] == 1, "single-layer trunk"
    d = x_prompt.shape[-1]
    mw = d // 2
    cg = (d - mw) // N_GROUPS
    w_in0 = w_in[0]
    w_main = jnp.concatenate([w_in0[:, :4 * mw], w_in0[:, 4 * mw + N_GATES:]], axis=1).astype(BF16)
    w_gates = jnp.pad(w_in0[:, 4 * mw:4 * mw + N_GATES], ((0, 0), (0, LANES - N_GATES))).astype(BF16)
    bg = jnp.pad(b_gates[0], (0, LANES - N_GATES)).reshape(1, LANES)
    kc = np.arange(cg)
    angc = 2.0 * np.pi * np.outer(kc, kc) / cg
    mc = jnp.asarray(np.concatenate([np.cos(angc), -np.sin(angc)], axis=0) / math.sqrt(cg), BF16)
    wr = w_router[0]
    wr_hi = wr.astype(BF16)
    wr_lo = (wr - wr_hi.astype(F32)).astype(BF16)
    n_exp = wr.shape[1]
    mxu_cols = 2 * LANES
    wr_cat = jnp.pad(jnp.concatenate([wr_hi, wr_lo], axis=1), ((0, 0), (0, max(0, mxu_cols - 2 * n_exp))))
    wts = (ln1_g[0], ln2_g[0], w_main, w_gates, bg, conv_w[0], conv_b[0], head_norm_g[0], w_four[0],
           w_out[0].astype(BF16), mc, wr_cat, b_router[0])
    nbp = c_prompt.shape[0]
    ada = _ada(jnp.concatenate([c_prompt, c_sample], axis=0), w_ada[0], b_ada[0])
    gp = _mixer_and_router(x_prompt, ada[:nbp], wts, jnp.zeros((1, n_exp), F32))
    gs = _mixer_and_router(x_sample, ada[nbp:], wts, gp["cnt"])
    groups = (gp, gs)

    counts = gs["cnt"][0].astype(jnp.int32)
    p_counts = (counts + MOE_BM - 1) // MOE_BM * MOE_BM
    p_ends = jnp.cumsum(p_counts)
    p_off = p_ends - p_counts
    n_tok = sum(g["x1"].shape[0] for g in groups)
    n_blocks = (n_tok * TOP_K) // MOE_BM + n_exp
    starts = jnp.arange(n_blocks, dtype=jnp.int32) * MOE_BM
    block_e = jnp.minimum(jnp.sum((starts[:, None] >= p_ends[None, :]).astype(jnp.int32), axis=1), n_exp - 1)
    block_valid = jnp.clip((p_off + counts)[block_e] - starts, 0, MOE_BM).astype(jnp.int32)
    dests = [(p_off[g["idx"]] + g["rank"]).reshape(-1) for g in groups]
    rows = n_blocks * MOE_BM
    gap_start = jnp.concatenate([p_off + counts, p_ends[-1:]])
    gap_len = jnp.concatenate([p_counts - counts, rows - p_ends[-1:]])
    gap_end = jnp.cumsum(gap_len)
    j = jnp.arange(n_exp * MOE_BM, dtype=jnp.int32)
    gi = jnp.sum((j[:, None] >= gap_end[None, :]).astype(jnp.int32), axis=1)
    pad_dest = (gap_start[gi] + (j - (gap_end - gap_len)[gi])).astype(jnp.int32)

    xs = _dispatch([g["h2p"] for g in groups], jnp.concatenate(dests + [pad_dest]), d // 2 // LANES)
    ys = _moe(xs, block_e, block_valid, w_gate_up[0].astype(BF16), b_gate_up[0], w_down[0].astype(BF16), b_down[0])
    outs = []
    for g, dest in zip(groups, dests):
        bsz, seq, _ = g["shape"]
        outs.append(_combine(g["x1"], g["gw"], g["g2"], final_g, ys, dest, seq).reshape(g["shape"]))
    return tuple(outs)
```
